```python
import math
import jax, jax.numpy as jnp
from jax import lax
import numpy as np

D_MODEL = 1024
BATCH = 4
SEQ = 4096
DEPTH = 4
DEC_BATCH = 32
DEC_SEQ = 4
PAST_LEN = 8192
PAGE_SIZE = 128

D_MIX = D_MODEL
W_A = D_MIX // 4
CONV_A = 31
H_B = 4
DK_B = 64
DV_B = 2 * DK_B
W_B = H_B * DV_B
H_C = 4
DK_C = 64
DV_C = 64
W_C = H_C * DV_C
QKV_C = H_C * (2 * DK_C + DV_C)
SHORT_CONV = 4
CHUNK = 64
Q_BLOCK = 128
EPS = 1e-6
SPLIT_SIZES = (W_A, W_A, W_A,
               H_B * 2 * DK_B, H_B * 2 * DK_B, W_B, W_B,
               QKV_C, H_C, H_C, W_C)
N_IN = sum(SPLIT_SIZES)

kernel_name = "hybrid_conv_diffattn_gdn_step"


def rmsnorm(x, w):
    xf = x.astype(jnp.float32)
    y = xf * lax.rsqrt(jnp.mean(xf * xf, -1, keepdims=True) + EPS)
    return (y * w.astype(jnp.float32)).astype(x.dtype)


def layernorm(x, w, b):
    xf = x.astype(jnp.float32)
    mu = jnp.mean(xf, -1, keepdims=True)
    xc = xf - mu
    y = xc * lax.rsqrt(jnp.mean(xc * xc, -1, keepdims=True) + EPS)
    return (y * w.astype(jnp.float32) + b.astype(jnp.float32)).astype(x.dtype)


def l2norm(x):
    return x * lax.rsqrt(jnp.sum(x * x, -1, keepdims=True) + EPS)


def causal_dwconv(ext, w):
    return lax.conv_general_dilated(ext, w[:, None, :].astype(ext.dtype), window_strides=(1,),
                                    padding="VALID", dimension_numbers=("NWC", "WIO", "NWC"),
                                    feature_group_count=ext.shape[-1])


def split_cols(p):
    outs, off = [], 0
    for s in SPLIT_SIZES:
        outs.append(p[..., off:off + s])
        off += s
    return outs


def diff_attend(q, k, v, q_pos, k_pos, lam):
    s = jnp.einsum('bqhmd,bkhmd->bhmqk', q, k, preferred_element_type=jnp.float32) * (DK_B ** -0.5)
    mask = k_pos[None, :] <= q_pos[:, None]
    p = jax.nn.softmax(jnp.where(mask, s, -jnp.inf), axis=-1)
    a = p[:, :, 0] - lam * p[:, :, 1]
    return jnp.einsum('bhqk,bkhd->bqhd', a, v.astype(jnp.float32))


def diff_attention(q, k, v, q_pos, k_pos, lam):
    bsz, lq = q.shape[:2]
    if lq > Q_BLOCK and lq % Q_BLOCK == 0:
        n = lq // Q_BLOCK
        qb = jnp.moveaxis(q.reshape((bsz, n, Q_BLOCK) + q.shape[2:]), 1, 0)
        pb = q_pos.reshape(n, Q_BLOCK)
        ob = lax.map(lambda a: diff_attend(a[0], k, v, a[1], k_pos, lam), (qb, pb))
        return jnp.moveaxis(ob, 0, 1).reshape(bsz, lq, H_B, DV_B)
    return diff_attend(q, k, v, q_pos, k_pos, lam)


def gated_delta(q, k, v, g, beta, s0):
    bsz, L = q.shape[:2]
    pad = (-L) % CHUNK
    n = (L + pad) // CHUNK

    def prep(t):
        t = jnp.pad(t, [(0, 0), (0, pad)] + [(0, 0)] * (t.ndim - 2))
        t = t.reshape((bsz, n, CHUNK) + t.shape[2:])
        return jnp.moveaxis(t, (1, 3), (0, 2))

    qc, kc, vc, gc, bc = prep(q), prep(k), prep(v), prep(g), prep(beta)
    G = jnp.cumsum(gc, axis=-1)
    idx = jnp.arange(CHUNK)
    tril = idx[:, None] >= idx[None, :]
    strict = idx[:, None] > idx[None, :]
    decay = jnp.exp(jnp.where(tril, G[..., :, None] - G[..., None, :], -jnp.inf))
    kk = jnp.einsum('nbhid,nbhjd->nbhij', kc, kc)
    m = jnp.eye(CHUNK, dtype=jnp.float32) + jnp.where(strict, bc[..., :, None] * kk * decay, 0.0)
    eg = jnp.exp(G)
    u = lax.linalg.triangular_solve(m, vc * bc[..., None], left_side=True, lower=True, unit_diagonal=True)
    w = lax.linalg.triangular_solve(m, kc * (bc * eg)[..., None], left_side=True, lower=True,
                                    unit_diagonal=True)
    qk = jnp.einsum('nbhid,nbhjd->nbhij', qc, kc) * decay
    qg = qc * eg[..., None]
    glast = G[..., -1:]
    kt = kc * jnp.exp(glast - G)[..., None]
    gl = jnp.exp(glast)[..., None]

    def step(S, xs):
        u_, w_, qk_, qg_, kt_, gl_ = xs
        vn = u_ - jnp.einsum('bhcd,bhde->bhce', w_, S)
        o = jnp.einsum('bhcd,bhde->bhce', qg_, S) + jnp.einsum('bhij,bhje->bhie', qk_, vn)
        S = S * gl_ + jnp.einsum('bhcd,bhce->bhde', kt_, vn)
        return S, o

    S, o = lax.scan(step, s0, (u, w, qk, qg, kt, gl))
    o = jnp.moveaxis(o, (0, 2), (1, 3)).reshape(bsz, n * CHUNK, H_C, DV_C)[:, :L]
    return o, S


def mixer_layer(h, lp, lam_init, conv_a_prev, conv_c_prev, s0, past_k, past_v):
    bsz, L, _ = h.shape
    hn = rmsnorm(h, lp['norm'])
    proj = hn @ lp['w_in']
    a_val, a_glu, a_gate, qb, kb, vb, b_gate, qkv_c, beta_raw, alpha_raw, c_gate = split_cols(proj)

    u = a_val * jax.nn.sigmoid(a_glu)
    ext_a = jnp.concatenate([conv_a_prev.astype(u.dtype), u], axis=1)
    ya = causal_dwconv(ext_a, lp['conv_a_w']) + lp['conv_a_b']
    ya = jax.nn.silu(layernorm(ya, lp['ln_a_w'], lp['ln_a_b']))
    ya = (ya @ lp['pw_a_w'] + lp['pw_a_b']) * jax.nn.silu(a_gate)
    new_conv_a = ext_a[:, -(CONV_A - 1):]

    q = qb.reshape(bsz, L, H_B, 2, DK_B)
    k = kb.reshape(bsz, L, H_B, 2, DK_B)
    v = vb.reshape(bsz, L, H_B, DV_B)
    if past_k is None:
        past_len, k_all, v_all = 0, k, v
    else:
        past_len = past_k.shape[1]
        k_all = jnp.concatenate([past_k.astype(k.dtype), k], axis=1)
        v_all = jnp.concatenate([past_v.astype(v.dtype), v], axis=1)
    q_pos = past_len + jnp.arange(L)
    k_pos = jnp.arange(past_len + L)
    f32 = jnp.float32
    lam = (jnp.exp(jnp.sum(lp['lq1'].astype(f32) * lp['lk1'].astype(f32)))
           - jnp.exp(jnp.sum(lp['lq2'].astype(f32) * lp['lk2'].astype(f32))) + lam_init)
    ob = diff_attention(q, k_all, v_all, q_pos, k_pos, lam)
    ob = rmsnorm(ob, lp['subln']) * (1.0 - lam_init)
    yb = ob.reshape(bsz, L, W_B).astype(h.dtype) * jax.nn.silu(b_gate)
    new_k = kb.reshape(bsz, L, H_B, 2 * DK_B)
    new_v = v

    ext_c = jnp.concatenate([conv_c_prev.astype(qkv_c.dtype), qkv_c], axis=1)
    qkv = jax.nn.silu(causal_dwconv(ext_c, lp['conv_c_w'])).astype(f32)
    new_conv_c = ext_c[:, -(SHORT_CONV - 1):]
    qc = l2norm(qkv[..., :H_C * DK_C].reshape(bsz, L, H_C, DK_C)) * (DK_C ** -0.5)
    kc = l2norm(qkv[..., H_C * DK_C:2 * H_C * DK_C].reshape(bsz, L, H_C, DK_C))
    vc = qkv[..., 2 * H_C * DK_C:].reshape(bsz, L, H_C, DV_C)
    beta = jax.nn.sigmoid(beta_raw.astype(f32))
    g = -jnp.exp(lp['a_log'].astype(f32)) * jax.nn.softplus(alpha_raw.astype(f32) + lp['dt_bias'].astype(f32))
    oc, s_new = gated_delta(qc, kc, vc, g, beta, s0.astype(f32))
    oc = rmsnorm(oc, lp['onorm_c'])
    yc = oc.reshape(bsz, L, W_C).astype(h.dtype) * jax.nn.silu(c_gate)

    y = jnp.concatenate([ya, yb.astype(ya.dtype), yc.astype(ya.dtype)], axis=-1) @ lp['w_out']
    return h + y, (new_k, new_v, new_conv_a, new_conv_c, s_new)


def setup_inputs(seed: int = 0) -> dict:
    key = jax.random.key(seed)
    ks = jax.random.split(key, 32)
    f32 = jnp.float32
    n_pages = PAST_LEN // PAGE_SIZE
    n_pool = (5 * DEC_BATCH * n_pages) // 4
    nrm = lambda k, shape, s: jax.random.normal(k, shape, f32) * s
    dt = jnp.exp(jax.random.uniform(ks[20], (DEPTH, H_C), f32, math.log(1e-3), math.log(1e-1)))
    return {
        "x_prompt": nrm(ks[0], (BATCH, SEQ, D_MODEL), 1.0),
        "x_sample": nrm(ks[1], (DEC_BATCH, DEC_SEQ, D_MODEL), 1.0),
        "cache_k": nrm(ks[2], (DEPTH, n_pool, PAGE_SIZE, H_B, 2 * DK_B), 1.0),
        "cache_v": nrm(ks[3], (DEPTH, n_pool, PAGE_SIZE, H_B, DV_B), 1.0),
        "state_conv_a": nrm(ks[4], (DEPTH, DEC_BATCH, CONV_A - 1, W_A), 1.0),
        "state_conv_c": nrm(ks[5], (DEPTH, DEC_BATCH, SHORT_CONV - 1, QKV_C), 1.0),
        "state_delta": nrm(ks[6], (DEPTH, DEC_BATCH, H_C, DK_C, DV_C), 0.5),
        "page_table": jax.random.permutation(ks[7], n_pool)[:DEC_BATCH * n_pages]
                      .reshape(DEC_BATCH, n_pages).astype(jnp.int32),
        "norm_w": 1.0 + nrm(ks[8], (DEPTH, D_MODEL), 0.01),
        "w_in": nrm(ks[9], (DEPTH, D_MODEL, N_IN), D_MODEL ** -0.5),
        "conv_a_w": nrm(ks[10], (DEPTH, CONV_A, W_A), CONV_A ** -0.5),
        "conv_a_b": nrm(ks[11], (DEPTH, W_A), 0.02),
        "ln_a_w": 1.0 + nrm(ks[12], (DEPTH, W_A), 0.01),
        "ln_a_b": nrm(ks[13], (DEPTH, W_A), 0.02),
        "pw_a_w": nrm(ks[14], (DEPTH, W_A, W_A), W_A ** -0.5),
        "pw_a_b": nrm(ks[15], (DEPTH, W_A), 0.02),
        "lambda_q1": nrm(ks[16], (DEPTH, DK_B), 0.1),
        "lambda_k1": nrm(ks[17], (DEPTH, DK_B), 0.1),
        "lambda_q2": nrm(ks[18], (DEPTH, DK_B), 0.1),
        "lambda_k2": nrm(ks[19], (DEPTH, DK_B), 0.1),
        "subln_w": 1.0 + nrm(ks[21], (DEPTH, DV_B), 0.01),
        "conv_c_w": nrm(ks[22], (DEPTH, SHORT_CONV, QKV_C), SHORT_CONV ** -0.5),
        "a_log": jnp.log(jax.random.uniform(ks[23], (DEPTH, H_C), f32, 1.0, 16.0)),
        "dt_bias": dt + jnp.log(-jnp.expm1(-dt)),
        "onorm_c_w": 1.0 + nrm(ks[24], (DEPTH, DV_C), 0.01),
        "w_out": nrm(ks[25], (DEPTH, D_MIX, D_MODEL), D_MIX ** -0.5),
        "final_norm_w": 1.0 + nrm(ks[26], (D_MODEL,), 0.01),
    }


def reference(x_prompt, x_sample, cache_k, cache_v, state_conv_a, state_conv_c, state_delta, page_table,
              norm_w, w_in, conv_a_w, conv_a_b, ln_a_w, ln_a_b, pw_a_w, pw_a_b,
              lambda_q1, lambda_k1, lambda_q2, lambda_k2, subln_w, conv_c_w, a_log, dt_bias,
              onorm_c_w, w_out, final_norm_w):
    n_pages = page_table.shape[1]
    past_len = n_pages * PAGE_SIZE
    hp, hs = x_prompt, x_sample
    outs_p, outs_s = [], []
    for l in range(DEPTH):
        lp = dict(norm=norm_w[l], w_in=w_in[l], conv_a_w=conv_a_w[l], conv_a_b=conv_a_b[l],
                  ln_a_w=ln_a_w[l], ln_a_b=ln_a_b[l], pw_a_w=pw_a_w[l], pw_a_b=pw_a_b[l],
                  lq1=lambda_q1[l], lk1=lambda_k1[l], lq2=lambda_q2[l], lk2=lambda_k2[l],
                  subln=subln_w[l], conv_c_w=conv_c_w[l], a_log=a_log[l], dt_bias=dt_bias[l],
                  onorm_c=onorm_c_w[l], w_out=w_out[l])
        lam_init = 0.8 - 0.6 * math.exp(-0.3 * l)
        hp, sp = mixer_layer(hp, lp, lam_init,
                             jnp.zeros((hp.shape[0], CONV_A - 1, W_A), hp.dtype),
                             jnp.zeros((hp.shape[0], SHORT_CONV - 1, QKV_C), hp.dtype),
                             jnp.zeros((hp.shape[0], H_C, DK_C, DV_C), jnp.float32), None, None)
        pk = cache_k[l][page_table].reshape(DEC_BATCH if False else page_table.shape[0], past_len, H_B, 2, DK_B)
        pv = cache_v[l][page_table].reshape(page_table.shape[0], past_len, H_B, DV_B)
        hs, ss = mixer_layer(hs, lp, lam_init, state_conv_a[l], state_conv_c[l], state_delta[l], pk, pv)
        outs_p.append(sp)
        outs_s.append(ss)
    y_prompt = rmsnorm(hp, final_norm_w)
    y_sample = rmsnorm(hs, final_norm_w)
    k_p = jnp.stack([o[0] for o in outs_p])
    v_p = jnp.stack([o[1] for o in outs_p])
    ca_p = jnp.stack([o[2] for o in outs_p])
    cc_p = jnp.stack([o[3] for o in outs_p])
    d_p = jnp.stack([o[4] for o in outs_p]).astype(state_delta.dtype)
    k_s = jnp.stack([o[0] for o in outs_s])
    v_s = jnp.stack([o[1] for o in outs_s])
    ca_s = jnp.stack([o[2] for o in outs_s])
    cc_s = jnp.stack([o[3] for o in outs_s])
    d_s = jnp.stack([o[4] for o in outs_s]).astype(state_delta.dtype)
    return (y_prompt, y_sample, k_p, v_p, ca_p, cc_p, d_p, k_s, v_s, ca_s, cc_s, d_s)
```

```python
import functools
import math

import jax
import jax.numpy as jnp
from jax import lax
from jax.experimental import pallas as pl
from jax.experimental.pallas import tpu as pltpu

F32 = jnp.float32
BF16 = jnp.bfloat16
HI = lax.Precision.HIGHEST

EPS = 1e-6
W_A = 256
CONV_A = 31
H_B = 4
DK_B = 64
DV_B = 128
W_B = H_B * DV_B
H_C = 4
DK_C = 64
DV_C = 64
W_C = H_C * DV_C
QKV_C = H_C * (2 * DK_C + DV_C)
SHORT_CONV = 4
CHUNK = 64
PAGE = 128
D_MIX = W_A + W_B + W_C

LANES = 128
SUBLANES = 8
SROWS = SUBLANES
VMEM_LIMIT = 48 * 1024 * 1024

C_A3 = 0
C_Q = C_A3 + 3 * W_A
C_K = C_Q + W_B
C_V = C_K + W_B
C_BG = C_V + W_B
C_QKVC = C_BG + W_B
C_CG = C_QKVC + QKV_C
C_BETA = C_CG + W_C
C_ALPHA = C_BETA + LANES
N_PROJ = C_ALPHA + LANES
W_CM = N_PROJ - C_CG


def _cparams(sem):
    return pltpu.CompilerParams(dimension_semantics=sem, vmem_limit_bytes=VMEM_LIMIT)


def _sigmoid(x):
    return 1.0 / (1.0 + jnp.exp(-x))


def _silu(x):
    return x * _sigmoid(x)


def _softplus(x):
    return jnp.maximum(x, 0.0) + jnp.log1p(jnp.exp(-jnp.abs(x)))


def _proj_kernel(x_ref, nw_ref, w_ref, a3_ref, q_ref, k32_ref, k16_ref, v32_ref, v16_ref,
                 bg_ref, qkvc_ref, cm_ref):
    x = x_ref[...]
    ms = jnp.mean(x * x, axis=-1, keepdims=True)
    hn = (x * lax.rsqrt(ms + EPS) * nw_ref[...]).astype(BF16)

    def mm(lo, hi):
        return jnp.dot(hn, w_ref[:, lo:hi], preferred_element_type=F32)

    a3_ref[...] = mm(C_A3, C_Q)
    q_ref[...] = (mm(C_Q, C_K) * (DK_B ** -0.5)).astype(BF16)
    k = mm(C_K, C_V)
    k32_ref[...] = k
    k16_ref[...] = k.astype(BF16)
    v = mm(C_V, C_BG)
    v32_ref[...] = v
    v16_ref[...] = v.astype(BF16)
    bg_ref[...] = mm(C_BG, C_QKVC)
    qkvc_ref[...] = mm(C_QKVC, C_CG)
    cm_ref[...] = mm(C_CG, N_PROJ)


def _proj(x, nw, w, tm):
    m, d = x.shape
    widths = [(3 * W_A, F32), (W_B, BF16), (W_B, F32), (W_B, BF16), (W_B, F32), (W_B, BF16),
              (W_B, F32), (QKV_C, F32), (W_CM, F32)]
    return pl.pallas_call(
        _proj_kernel,
        grid=(m // tm,),
        in_specs=[pl.BlockSpec((tm, d), lambda i: (i, 0)),
                  pl.BlockSpec((1, d), lambda i: (0, 0)),
                  pl.BlockSpec((d, N_PROJ), lambda i: (0, 0))],
        out_specs=[pl.BlockSpec((tm, wd), lambda i: (i, 0)) for wd, _ in widths],
        out_shape=[jax.ShapeDtypeStruct((m, wd), dt) for wd, dt in widths],
        compiler_params=_cparams(("parallel",)),
        name="proj",
    )(x, nw, w)


def _outproj_kernel(ya_ref, yb_ref, yc_ref, h_ref, w_ref, fw_ref, o_ref, *, final):
    ycat = jnp.concatenate([ya_ref[...].astype(BF16), yb_ref[...].astype(BF16),
                            yc_ref[...].astype(BF16)], axis=-1)
    hnew = h_ref[...] + jnp.dot(ycat, w_ref[...], preferred_element_type=F32)
    if final:
        ms = jnp.mean(hnew * hnew, axis=-1, keepdims=True)
        hnew = hnew * lax.rsqrt(ms + EPS) * fw_ref[...]
    o_ref[...] = hnew


def _outproj(ya, yb, yc, h, w, fw, tm, final):
    m, d = h.shape
    row = lambda wd: pl.BlockSpec((tm, wd), lambda i: (i, 0))
    return pl.pallas_call(
        functools.partial(_outproj_kernel, final=final),
        grid=(m // tm,),
        in_specs=[row(W_A), row(W_B), row(W_C), row(d),
                  pl.BlockSpec((D_MIX, d), lambda i: (0, 0)),
                  pl.BlockSpec((1, d), lambda i: (0, 0))],
        out_specs=row(d),
        out_shape=jax.ShapeDtypeStruct((m, d), F32),
        compiler_params=_cparams(("parallel",)),
        name="outproj",
    )(ya, yb, yc, h, w, fw)


def _conva_tail(conv, a_gate, cb, lnw, lnb, pw, pb):
    y = conv + cb
    mu = jnp.mean(y, axis=-1, keepdims=True)
    yc = y - mu
    y = yc * lax.rsqrt(jnp.mean(yc * yc, axis=-1, keepdims=True) + EPS) * lnw + lnb
    y = _silu(y)
    y = jnp.dot(y.astype(BF16), pw, preferred_element_type=F32) + pb
    return y * _silu(a_gate)


CONV_PRE = 32
CONV_OFF = CONV_PRE - (CONV_A - 1)
CONV_RC = 64


def _conva_prompt_kernel(a3_ref, cw_ref, cb_ref, lnw_ref, lnb_ref, pw_ref, pb_ref,
                         ya_ref, tail_ref, ext_ref, *, tl):
    i = pl.program_id(1)

    @pl.when(i == 0)
    def _():
        ext_ref[0:CONV_PRE, :] = jnp.zeros((CONV_PRE, W_A), F32)

    u = a3_ref[:, 0:W_A] * _sigmoid(a3_ref[:, W_A:2 * W_A])
    ext_ref[CONV_PRE:CONV_PRE + tl, :] = u
    for r in range(tl // CONV_RC):
        base = r * CONV_RC + CONV_OFF
        acc = jnp.zeros((CONV_RC, W_A), F32)
        for j in range(CONV_A):
            acc = acc + cw_ref[j:j + 1, :] * ext_ref[base + j:base + j + CONV_RC, :]
        gate = a3_ref[r * CONV_RC:(r + 1) * CONV_RC, 2 * W_A:3 * W_A]
        y = _conva_tail(acc, gate, cb_ref[...], lnw_ref[...], lnb_ref[...], pw_ref[...], pb_ref[...])
        ya_ref[r * CONV_RC:(r + 1) * CONV_RC, :] = y.astype(BF16)
    last = ext_ref[tl:tl + CONV_PRE, :]
    tail_ref[...] = last
    ext_ref[0:CONV_PRE, :] = last


def _conva_prompt(a3, cw, cb, lnw, lnb, pw, pb, bsz, seq, tl):
    nl = seq // tl
    full = lambda shp: pl.BlockSpec(shp, lambda b, i: (0,) * len(shp))
    return pl.pallas_call(
        functools.partial(_conva_prompt_kernel, tl=tl),
        grid=(bsz, nl),
        in_specs=[pl.BlockSpec((tl, 3 * W_A), lambda b, i: (b * nl + i, 0)),
                  full((CONV_A, W_A)), full((1, W_A)), full((1, W_A)), full((1, W_A)),
                  full((W_A, W_A)), full((1, W_A))],
        out_specs=[pl.BlockSpec((tl, W_A), lambda b, i: (b * nl + i, 0)),
                   pl.BlockSpec((None, CONV_PRE, W_A), lambda b, i: (b, 0, 0))],
        out_shape=[jax.ShapeDtypeStruct((bsz * seq, W_A), BF16),
                   jax.ShapeDtypeStruct((bsz, CONV_PRE, W_A), F32)],
        scratch_shapes=[pltpu.VMEM((CONV_PRE + tl, W_A), F32)],
        compiler_params=_cparams(("parallel", "arbitrary")),
        name="conva_prompt",
    )(a3, cw, cb, lnw, lnb, pw, pb)


SEQ_GROUP = 8


def _conva_sample_kernel(a3_ref, st_ref, cw_ref, cb_ref, lnw_ref, lnb_ref, pw_ref, pb_ref,
                         ya_ref, u_ref, ext_ref):
    a3 = a3_ref[...]
    u = a3[:, :, 0:W_A] * _sigmoid(a3[:, :, W_A:2 * W_A])
    u_ref[...] = u
    ext_ref[:, 0:CONV_PRE, :] = st_ref[...]
    ext_ref[:, CONV_PRE:CONV_PRE + SROWS, :] = u
    acc = jnp.zeros((SEQ_GROUP, SROWS, W_A), F32)
    for j in range(CONV_A):
        acc = acc + cw_ref[j:j + 1, :] * ext_ref[:, CONV_OFF + j:CONV_OFF + j + SROWS, :]
    rows = SEQ_GROUP * SROWS
    y = _conva_tail(acc.reshape(rows, W_A), a3[:, :, 2 * W_A:3 * W_A].reshape(rows, W_A),
                    cb_ref[...], lnw_ref[...], lnb_ref[...], pw_ref[...], pb_ref[...])
    ya_ref[...] = y.reshape(SEQ_GROUP, SROWS, W_A)


def _conva_sample(a3, st, cw, cb, lnw, lnb, pw, pb):
    nb = a3.shape[0]
    full = lambda shp: pl.BlockSpec(shp, lambda g: (0,) * len(shp))
    seqs = lambda r, wd: pl.BlockSpec((SEQ_GROUP, r, wd), lambda g: (g, 0, 0))
    return pl.pallas_call(
        _conva_sample_kernel,
        grid=(nb // SEQ_GROUP,),
        in_specs=[seqs(SROWS, 3 * W_A), seqs(CONV_PRE, W_A),
                  full((CONV_A, W_A)), full((1, W_A)), full((1, W_A)), full((1, W_A)),
                  full((W_A, W_A)), full((1, W_A))],
        out_specs=[seqs(SROWS, W_A), seqs(SROWS, W_A)],
        out_shape=[jax.ShapeDtypeStruct((nb, SROWS, W_A), F32),
                   jax.ShapeDtypeStruct((nb, SROWS, W_A), F32)],
        scratch_shapes=[pltpu.VMEM((SEQ_GROUP, CONV_PRE + SROWS, W_A), F32)],
        compiler_params=_cparams(("parallel",)),
        name="conva_sample",
    )(a3, st, cw, cb, lnw, lnb, pw, pb)


def _lambda_full(lq1, lk1, lq2, lk2, lam_init):
    s1 = jnp.sum(lq1 * lk1, axis=-1, keepdims=True)
    s2 = jnp.sum(lq2 * lk2, axis=-1, keepdims=True)
    return jnp.exp(s1) - jnp.exp(s2) + lam_init


def _subln_gate(o, sub, bg, lam_init):
    y = o * lax.rsqrt(jnp.mean(o * o, axis=-1, keepdims=True) + EPS) * sub
    return (y * (1.0 - lam_init)) * _silu(bg)


def _online_update(s, v, m_ref, l_ref, acc_ref):
    m_prev = m_ref[...]
    m_new = jnp.maximum(m_prev, jnp.max(s, axis=-1, keepdims=True))
    alpha = jnp.exp(m_prev - m_new)
    p = jnp.exp(s - m_new)
    l_ref[...] = alpha * l_ref[...] + jnp.sum(p, axis=-1, keepdims=True)
    acc_ref[...] = alpha * acc_ref[...] + jnp.dot(p.astype(BF16), v, preferred_element_type=F32)
    m_ref[...] = m_new


def _attn_prompt_kernel(q_ref, k_ref, v_ref, bg_ref, sub_ref, lq1_ref, lk1_ref, lq2_ref, lk2_ref,
                        o_ref, q1_ref, q2_ref, m1_ref, l1_ref, a1_ref, m2_ref, l2_ref, a2_ref,
                        *, lam_init, t):
    qi = pl.program_id(2)
    ki = pl.program_id(3)

    @pl.when(ki == 0)
    def _():
        q = q_ref[...]
        lane = lax.broadcasted_iota(jnp.int32, q.shape, 1)
        q1_ref[...] = jnp.where(lane < DK_B, q, jnp.zeros_like(q))
        q2_ref[...] = jnp.where(lane >= DK_B, q, jnp.zeros_like(q))
        for m_ref, l_ref, a_ref in ((m1_ref, l1_ref, a1_ref), (m2_ref, l2_ref, a2_ref)):
            m_ref[...] = jnp.full(m_ref.shape, -jnp.inf, F32)
            l_ref[...] = jnp.zeros(l_ref.shape, F32)
            a_ref[...] = jnp.zeros(a_ref.shape, F32)

    @pl.when(ki <= qi)
    def _():
        k = k_ref[...]
        v = v_ref[...]
        row = lax.broadcasted_iota(jnp.int32, (t, t), 0)
        col = lax.broadcasted_iota(jnp.int32, (t, t), 1)
        keep = jnp.logical_or(col <= row, ki < qi)
        for qm_ref, m_ref, l_ref, a_ref in ((q1_ref, m1_ref, l1_ref, a1_ref),
                                            (q2_ref, m2_ref, l2_ref, a2_ref)):
            s = lax.dot_general(qm_ref[...], k, (((1,), (1,)), ((), ())),
                                preferred_element_type=F32)
            s = jnp.where(keep, s, -jnp.inf)
            _online_update(s, v, m_ref, l_ref, a_ref)

    @pl.when(ki == qi)
    def _():
        lam = _lambda_full(lq1_ref[...], lk1_ref[...], lq2_ref[...], lk2_ref[...], lam_init)
        o = a1_ref[...] / l1_ref[...] - lam * (a2_ref[...] / l2_ref[...])
        o_ref[...] = _subln_gate(o, sub_ref[...], bg_ref[...], lam_init).astype(BF16)


def _attn_prompt(q, k, v, bg, sub, lq1, lk1, lq2, lk2, lam_init, bsz, seq, t):
    n = seq // t
    qmap = lambda b, h, qi, ki: (b * n + qi, h)
    kmap = lambda b, h, qi, ki: (b * n + jnp.minimum(ki, qi), h)
    full = lambda shp: pl.BlockSpec(shp, lambda b, h, qi, ki: (0,) * len(shp))
    return pl.pallas_call(
        functools.partial(_attn_prompt_kernel, lam_init=lam_init, t=t),
        grid=(bsz, H_B, n, n),
        in_specs=[pl.BlockSpec((t, DV_B), qmap), pl.BlockSpec((t, DV_B), kmap),
                  pl.BlockSpec((t, DV_B), kmap), pl.BlockSpec((t, DV_B), qmap),
                  full((1, DV_B)), full((1, DK_B)), full((1, DK_B)), full((1, DK_B)),
                  full((1, DK_B))],
        out_specs=pl.BlockSpec((t, DV_B), qmap),
        out_shape=jax.ShapeDtypeStruct((bsz * seq, W_B), BF16),
        scratch_shapes=[pltpu.VMEM((t, DV_B), BF16), pltpu.VMEM((t, DV_B), BF16),
                        pltpu.VMEM((t, 1), F32), pltpu.VMEM((t, 1), F32), pltpu.VMEM((t, DV_B), F32),
                        pltpu.VMEM((t, 1), F32), pltpu.VMEM((t, 1), F32), pltpu.VMEM((t, DV_B), F32)],
        compiler_params=_cparams(("parallel", "parallel", "parallel", "arbitrary")),
        name="attn_prompt",
    )(q, k, v, bg, sub, lq1, lk1, lq2, lk2)


N_QROWS = 2 * H_B * SROWS


def _attn_sample_kernel(pt_ref, q_ref, kp_ref, vp_ref, kn_ref, vn_ref, bg_ref, sub_ref,
                        lq1_ref, lk1_ref, lq2_ref, lk2_ref, o_ref,
                        qb_ref, m_ref, l_ref, acc_ref, *, lam_init, n_pages):
    p = pl.program_id(1)

    @pl.when(p == 0)
    def _():
        q = q_ref[...]
        lane = lax.broadcasted_iota(jnp.int32, q.shape, 1)
        for mp in range(2):
            for h in range(H_B):
                lo = h * DV_B + mp * DK_B
                sel = jnp.logical_and(lane >= lo, lane < lo + DK_B)
                r0 = (mp * H_B + h) * SROWS
                qb_ref[r0:r0 + SROWS, :] = jnp.where(sel, q, 0.0)
        m_ref[...] = jnp.full(m_ref.shape, -jnp.inf, F32)
        l_ref[...] = jnp.zeros(l_ref.shape, F32)
        acc_ref[...] = jnp.zeros(acc_ref.shape, F32)

    def scores(k):
        return lax.dot_general(qb_ref[...].astype(BF16), k.astype(BF16), (((1,), (1,)), ((), ())),
                               preferred_element_type=F32)

    @pl.when(p < n_pages)
    def _():
        s = scores(kp_ref[...])
        _online_update(s, vp_ref[...].astype(BF16), m_ref, l_ref, acc_ref)

    @pl.when(p == n_pages)
    def _():
        s = scores(kn_ref[...])
        tq = lax.broadcasted_iota(jnp.int32, s.shape, 0) % SROWS
        tk = lax.broadcasted_iota(jnp.int32, s.shape, 1)
        s = jnp.where(tk <= tq, s, -jnp.inf)
        _online_update(s, vn_ref[...].astype(BF16), m_ref, l_ref, acc_ref)
        lam = _lambda_full(lq1_ref[...], lk1_ref[...], lq2_ref[...], lk2_ref[...], lam_init)
        an = acc_ref[...] / l_ref[...]
        outs = []
        for h in range(H_B):
            r1 = h * SROWS
            r2 = (H_B + h) * SROWS
            o = (an[r1:r1 + SROWS, h * DV_B:(h + 1) * DV_B]
                 - lam * an[r2:r2 + SROWS, h * DV_B:(h + 1) * DV_B])
            outs.append(_subln_gate(o, sub_ref[...], bg_ref[:, h * DV_B:(h + 1) * DV_B], lam_init))
        o_ref[...] = jnp.concatenate(outs, axis=-1)


def _attn_sample(page_table, q, cache_k, cache_v, kn, vn, bg, sub, lq1, lk1, lq2, lk2, lam_init, layer):
    nb, n_pages = page_table.shape
    seq = lambda b, p, pt: (b, 0)
    page = lambda b, p, pt: (layer, pt[b, jnp.minimum(p, n_pages - 1)], 0, 0)
    full = lambda shp: pl.BlockSpec(shp, lambda b, p, pt: (0,) * len(shp))
    grid_spec = pltpu.PrefetchScalarGridSpec(
        num_scalar_prefetch=1,
        grid=(nb, n_pages + 1),
        in_specs=[pl.BlockSpec((SROWS, W_B), seq),
                  pl.BlockSpec((None, None, PAGE, W_B), page),
                  pl.BlockSpec((None, None, PAGE, W_B), page),
                  pl.BlockSpec((SROWS, W_B), seq), pl.BlockSpec((SROWS, W_B), seq),
                  pl.BlockSpec((SROWS, W_B), seq),
                  full((1, DV_B)), full((1, DK_B)), full((1, DK_B)), full((1, DK_B)),
                  full((1, DK_B))],
        out_specs=pl.BlockSpec((SROWS, W_B), seq),
        scratch_shapes=[pltpu.VMEM((N_QROWS, W_B), F32), pltpu.VMEM((N_QROWS, 1), F32),
                        pltpu.VMEM((N_QROWS, 1), F32), pltpu.VMEM((N_QROWS, W_B), F32)],
    )
    return pl.pallas_call(
        functools.partial(_attn_sample_kernel, lam_init=lam_init, n_pages=n_pages),
        grid_spec=grid_spec,
        out_shape=jax.ShapeDtypeStruct((nb * SROWS, W_B), F32),
        compiler_params=_cparams(("parallel", "arbitrary")),
        name="attn_sample",
    )(page_table, q, cache_k, cache_v, kn, vn, bg, sub, lq1, lk1, lq2, lk2)


def _dot_hi(a, b):
    return jnp.dot(a, b, preferred_element_type=F32, precision=HI)


def _tri_inverse(a, c):
    row = lax.broadcasted_iota(jnp.int32, (c, c), 0)
    col = lax.broadcasted_iota(jnp.int32, (c, c), 1)
    eye = (row == col).astype(F32)
    nb = min(c, 16)
    ad = jnp.where(row // nb == col // nb, a, 0.0)
    t = eye - ad
    pw = ad
    size = 2
    while size < nb:
        pw = _dot_hi(pw, pw)
        t = t + _dot_hi(t, pw)
        size *= 2
    size = nb * 2
    while size <= c:
        off = jnp.logical_and(row // size == col // size, row // (size // 2) != col // (size // 2))
        t = t - _dot_hi(t, _dot_hi(jnp.where(off, a, 0.0), t))
        size *= 2
    return t


def _dot_nt(a, b):
    return lax.dot_general(a, b, (((1,), (1,)), ((), ())), preferred_element_type=F32)


def _dot_tn(a, b):
    return lax.dot_general(a, b, (((0,), (0,)), ((), ())), preferred_element_type=F32)


def _delta_chunk(q, k, v, gcol, grow, bcol, s, c):
    row = lax.broadcasted_iota(jnp.int32, (c, c), 0)
    col = lax.broadcasted_iota(jnp.int32, (c, c), 1)
    decay = jnp.exp(jnp.where(row >= col, gcol - grow, -jnp.inf))
    kb = k.astype(BF16)
    kk = _dot_nt(kb, kb)
    a = jnp.where(row > col, bcol * kk * decay, 0.0)
    t = _tri_inverse(a, c)
    eg = jnp.exp(gcol)
    uw = _dot_hi(t, jnp.concatenate([v * bcol, k * (bcol * eg)], axis=-1))
    u = uw[:, :DV_C]
    w = uw[:, DV_C:]
    qk = _dot_nt(q.astype(BF16), kb) * decay
    qg = q * eg
    glast = gcol[c - 1:c, :]
    kt = k * jnp.exp(glast - gcol)
    sb = s.astype(BF16)
    vn = u - jnp.dot(w.astype(BF16), sb, preferred_element_type=F32)
    vnb = vn.astype(BF16)
    o = (jnp.dot(qg.astype(BF16), sb, preferred_element_type=F32)
         + jnp.dot(qk.astype(BF16), vnb, preferred_element_type=F32))
    s_new = s * jnp.exp(glast) + _dot_tn(kt.astype(BF16), vnb)
    return o, s_new


def _l2norm(x):
    return x * lax.rsqrt(jnp.sum(x * x, axis=-1, keepdims=True) + EPS)


def _gdn_block(x, cm, alog, dtb, onw, s_list, c, valid):
    beta = _sigmoid(cm[:, W_C:W_C + LANES])
    g = -jnp.exp(alog) * _softplus(cm[:, W_C + LANES:W_C + 2 * LANES] + dtb)
    if valid is not None:
        beta = jnp.where(valid, beta, 0.0)
        g = jnp.where(valid, g, 0.0)
        x = jnp.where(valid, x, 0.0)
    row = lax.broadcasted_iota(jnp.int32, (c, c), 0)
    col = lax.broadcasted_iota(jnp.int32, (c, c), 1)
    gcs = _dot_hi((row >= col).astype(F32), g)
    gcs_t = lax.dot_general(g, (row <= col).astype(F32), (((0,), (0,)), ((), ())),
                            preferred_element_type=F32, precision=HI)
    ys, s_out = [], []
    for h in range(H_C):
        q = _l2norm(x[:, h * DK_C:(h + 1) * DK_C]) * (DK_C ** -0.5)
        k = _l2norm(x[:, W_C + h * DK_C:W_C + (h + 1) * DK_C])
        v = x[:, 2 * W_C + h * DV_C:2 * W_C + (h + 1) * DV_C]
        o, s_new = _delta_chunk(q, k, v, gcs[:, h:h + 1], gcs_t[h:h + 1, :], beta[:, h:h + 1],
                                s_list[h], c)
        o = o * lax.rsqrt(jnp.mean(o * o, axis=-1, keepdims=True) + EPS) * onw
        ys.append(o * _silu(cm[:, h * DV_C:(h + 1) * DV_C]))
        s_out.append(s_new)
    return jnp.concatenate(ys, axis=-1), s_out


CONVC_PRE = SUBLANES
CONVC_OFF = CONVC_PRE - (SHORT_CONV - 1)


def _gdn_prompt_kernel(x_ref, cm_ref, cw_ref, alog_ref, dtb_ref, onw_ref, y_ref, s_ref,
                       ext_ref, st_ref):
    i = pl.program_id(1)

    @pl.when(i == 0)
    def _():
        ext_ref[0:CONVC_PRE, :] = jnp.zeros((CONVC_PRE, QKV_C), F32)
        st_ref[...] = jnp.zeros(st_ref.shape, F32)

    ext_ref[CONVC_PRE:CONVC_PRE + CHUNK, :] = x_ref[...]
    acc = jnp.zeros((CHUNK, QKV_C), F32)
    for j in range(SHORT_CONV):
        acc = acc + cw_ref[j:j + 1, :] * ext_ref[CONVC_OFF + j:CONVC_OFF + j + CHUNK, :]
    ext_ref[0:CONVC_PRE, :] = ext_ref[CHUNK:CHUNK + CONVC_PRE, :]
    y, s_new = _gdn_block(_silu(acc), cm_ref[...], alog_ref[...], dtb_ref[...], onw_ref[...],
                          [st_ref[h] for h in range(H_C)], CHUNK, None)
    y_ref[...] = y.astype(BF16)
    for h in range(H_C):
        st_ref[h] = s_new[h]
    s_ref[...] = st_ref[...]


def _gdn_prompt(x, cm, cw, alog, dtb, onw, bsz, seq):
    nc = seq // CHUNK
    full = lambda shp: pl.BlockSpec(shp, lambda b, i: (0,) * len(shp))
    tok = lambda wd: pl.BlockSpec((CHUNK, wd), lambda b, i: (b * nc + i, 0))
    return pl.pallas_call(
        _gdn_prompt_kernel,
        grid=(bsz, nc),
        in_specs=[tok(QKV_C), tok(W_CM), full((SHORT_CONV, QKV_C)), full((1, LANES)),
                  full((1, LANES)), full((1, DV_C))],
        out_specs=[tok(W_C), pl.BlockSpec((None, H_C, DK_C, DV_C), lambda b, i: (b, 0, 0, 0))],
        out_shape=[jax.ShapeDtypeStruct((bsz * seq, W_C), BF16),
                   jax.ShapeDtypeStruct((bsz, H_C, DK_C, DV_C), F32)],
        scratch_shapes=[pltpu.VMEM((CONVC_PRE + CHUNK, QKV_C), F32),
                        pltpu.VMEM((H_C, DK_C, DV_C), F32)],
        compiler_params=_cparams(("parallel", "arbitrary")),
        name="gdn_prompt",
    )(x, cm, cw, alog, dtb, onw)


def _gdn_sample_kernel(x_ref, st_ref, cm_ref, s0_ref, cw_ref, alog_ref, dtb_ref, onw_ref,
                       y_ref, s_ref, *, n_real):
    ext = jnp.concatenate([st_ref[...], x_ref[...]], axis=0)
    acc = jnp.zeros((SROWS, QKV_C), F32)
    for j in range(SHORT_CONV):
        acc = acc + cw_ref[j:j + 1, :] * ext[CONVC_OFF + j:CONVC_OFF + j + SROWS, :]
    valid = lax.broadcasted_iota(jnp.int32, (SROWS, 1), 0) < n_real
    y, s_new = _gdn_block(_silu(acc), cm_ref[...], alog_ref[...], dtb_ref[...], onw_ref[...],
                          [s0_ref[h] for h in range(H_C)], SROWS, valid)
    y_ref[...] = y
    for h in range(H_C):
        s_ref[h] = s_new[h]


def _gdn_sample(x, st, cm, s0, cw, alog, dtb, onw, n_real):
    nb = s0.shape[0]
    full = lambda shp: pl.BlockSpec(shp, lambda b: (0,) * len(shp))
    tok = lambda wd: pl.BlockSpec((SROWS, wd), lambda b: (b, 0))
    state = pl.BlockSpec((None, H_C, DK_C, DV_C), lambda b: (b, 0, 0, 0))
    return pl.pallas_call(
        functools.partial(_gdn_sample_kernel, n_real=n_real),
        grid=(nb,),
        in_specs=[tok(QKV_C), pl.BlockSpec((None, CONVC_PRE, QKV_C), lambda b: (b, 0, 0)),
                  tok(W_CM), state, full((SHORT_CONV, QKV_C)), full((1, LANES)), full((1, LANES)),
                  full((1, DV_C))],
        out_specs=[tok(W_C), state],
        out_shape=[jax.ShapeDtypeStruct((nb * SROWS, W_C), F32),
                   jax.ShapeDtypeStruct((nb, H_C, DK_C, DV_C), F32)],
        compiler_params=_cparams(("parallel",)),
        name="gdn_sample",
    )(x, st, cm, s0, cw, alog, dtb, onw)


def _rearrange_w_in(w_in):
    depth, d, _ = w_in.shape
    n_main = 3 * W_A + 4 * W_B + QKV_C
    beta = w_in[:, :, n_main:n_main + H_C]
    alpha = w_in[:, :, n_main + H_C:n_main + 2 * H_C]
    gate = w_in[:, :, n_main + 2 * H_C:]
    zpad = jnp.zeros((depth, d, LANES - H_C), w_in.dtype)
    return jnp.concatenate([w_in[:, :, :n_main], gate, beta, zpad, alpha, zpad], axis=-1).astype(BF16)


def _pad_lanes(x):
    return jnp.pad(x, ((0, 0), (0, LANES - x.shape[-1])))


def kernel(x_prompt, x_sample, cache_k, cache_v, state_conv_a, state_conv_c, state_delta, page_table,
           norm_w, w_in, conv_a_w, conv_a_b, ln_a_w, ln_a_b, pw_a_w, pw_a_b,
           lambda_q1, lambda_k1, lambda_q2, lambda_k2, subln_w, conv_c_w, a_log, dt_bias,
           onorm_c_w, w_out, final_norm_w):
    bsz, seq, d = x_prompt.shape
    nb, ns, _ = x_sample.shape
    depth = w_in.shape[0]
    n_pool = cache_k.shape[1]
    t_attn = 512 if seq % 512 == 0 else seq
    tl_conv = t_attn
    tm_p = 256
    assert SHORT_CONV - 1 <= ns <= SROWS, "sample tokens must fit one 8-row tile"
    assert nb % SEQ_GROUP == 0 and seq % CHUNK == 0 and seq % t_attn == 0 and (bsz * seq) % tm_p == 0
    assert tl_conv % CONV_RC == 0 and tl_conv >= CONV_PRE

    wp = _rearrange_w_in(w_in)
    wo = w_out.astype(BF16)
    pw = pw_a_w.astype(BF16)
    ck = cache_k.reshape(depth, n_pool, PAGE, W_B)
    cv = cache_v.reshape(depth, n_pool, PAGE, W_B)
    st_a = jnp.pad(state_conv_a, ((0, 0), (0, 0), (CONV_PRE - (CONV_A - 1), 0), (0, 0)))
    st_c = jnp.pad(state_conv_c, ((0, 0), (0, 0), (CONVC_PRE - (SHORT_CONV - 1), 0), (0, 0)))
    alog = _pad_lanes(a_log)
    dtb = _pad_lanes(dt_bias)
    fw = final_norm_w.reshape(1, d)

    hp = x_prompt.reshape(bsz * seq, d)
    hs = jnp.pad(x_sample, ((0, 0), (0, SROWS - ns), (0, 0))).reshape(nb * SROWS, d)
    outs_p = [[] for _ in range(5)]
    outs_s = [[] for _ in range(5)]
    for l in range(depth):
        lam_init = 0.8 - 0.6 * math.exp(-0.3 * l)
        final = l == depth - 1
        row = lambda a: a[l].reshape(1, -1)
        lams = (row(lambda_q1), row(lambda_k1), row(lambda_q2), row(lambda_k2))

        a3, q, k32, k16, v32, v16, bg, qkvc, cm = _proj(hp, row(norm_w), wp[l], tm_p)
        ya, tail = _conva_prompt(a3, conv_a_w[l], row(conv_a_b), row(ln_a_w), row(ln_a_b), pw[l],
                                 row(pw_a_b), bsz, seq, tl_conv)
        yb = _attn_prompt(q, k16, v16, bg, row(subln_w), *lams, lam_init, bsz, seq, t_attn)
        yc, s_new = _gdn_prompt(qkvc, cm, conv_c_w[l], alog[l:l + 1], dtb[l:l + 1], row(onorm_c_w),
                                bsz, seq)
        hp = _outproj(ya, yb, yc, hp, wo[l], fw, tm_p, final)
        outs_p[0].append(k32.reshape(bsz, seq, H_B, 2 * DK_B))
        outs_p[1].append(v32.reshape(bsz, seq, H_B, DV_B))
        outs_p[2].append(tail[:, CONV_PRE - (CONV_A - 1):])
        outs_p[3].append(qkvc.reshape(bsz, seq, QKV_C)[:, seq - (SHORT_CONV - 1):])
        outs_p[4].append(s_new)

        a3, q, k32, k16, v32, v16, bg, qkvc, cm = _proj(hs, row(norm_w), wp[l], nb * SROWS)
        ya, u = _conva_sample(a3.reshape(nb, SROWS, 3 * W_A), st_a[l], conv_a_w[l], row(conv_a_b),
                              row(ln_a_w), row(ln_a_b), pw[l], row(pw_a_b))
        yb = _attn_sample(page_table, q.astype(F32), ck, cv, k32, v32, bg, row(subln_w), *lams,
                          lam_init, l)
        yc, s_new = _gdn_sample(qkvc, st_c[l], cm, state_delta[l], conv_c_w[l], alog[l:l + 1],
                                dtb[l:l + 1], row(onorm_c_w), ns)
        hs = _outproj(ya.reshape(nb * SROWS, W_A), yb, yc, hs, wo[l], fw, nb * SROWS, final)
        outs_s[0].append(k32.reshape(nb, SROWS, H_B, 2 * DK_B)[:, :ns])
        outs_s[1].append(v32.reshape(nb, SROWS, H_B, DV_B)[:, :ns])
        outs_s[2].append(jnp.concatenate([state_conv_a[l], u[:, :ns]], axis=1)[:, -(CONV_A - 1):])
        outs_s[3].append(jnp.concatenate([state_conv_c[l], qkvc.reshape(nb, SROWS, QKV_C)[:, :ns]],
                                         axis=1)[:, -(SHORT_CONV - 1):])
        outs_s[4].append(s_new)

    y_prompt = hp.reshape(bsz, seq, d)
    y_sample = hs.reshape(nb, SROWS, d)[:, :ns]
    stack = lambda xs: jnp.stack(xs)
    return (y_prompt, y_sample,
            stack(outs_p[0]), stack(outs_p[1]), stack(outs_p[2]), stack(outs_p[3]),
            stack(outs_p[4]).astype(state_delta.dtype),
            stack(outs_s[0]), stack(outs_s[1]), stack(outs_s[2]), stack(outs_s[3]),
            stack(outs_s[4]).astype(state_delta.dtype))
```

```python
import functools
import math

import jax
import jax.numpy as jnp
from jax import lax
from jax.experimental import pallas as pl
from jax.experimental.pallas import tpu as pltpu

F32 = jnp.float32
BF16 = jnp.bfloat16
HI = lax.Precision.HIGHEST

EPS = 1e-6
W_A = 256
CONV_A = 31
H_B = 4
DK_B = 64
DV_B = 128
W_B = H_B * DV_B
H_C = 4
DK_C = 64
DV_C = 64
W_C = H_C * DV_C
QKV_C = H_C * (2 * DK_C + DV_C)
SHORT_CONV = 4
CHUNK = 64
PAGE = 128
D_MIX = W_A + W_B + W_C

LANES = 128
SUBLANES = 8
SROWS = SUBLANES
VMEM_LIMIT = 48 * 1024 * 1024

C_A3 = 0
C_Q = C_A3 + 3 * W_A
C_K = C_Q + W_B
C_V = C_K + W_B
C_BG = C_V + W_B
C_QKVC = C_BG + W_B
C_CG = C_QKVC + QKV_C
C_BETA = C_CG + W_C
C_ALPHA = C_BETA + LANES
N_PROJ = C_ALPHA + LANES
W_CM = N_PROJ - C_CG


def _cparams(sem):
    return pltpu.CompilerParams(dimension_semantics=sem, vmem_limit_bytes=VMEM_LIMIT)


def _sigmoid(x):
    return 1.0 / (1.0 + jnp.exp(-x))


def _silu(x):
    return x * _sigmoid(x)


def _softplus(x):
    return jnp.maximum(x, 0.0) + jnp.log1p(jnp.exp(-jnp.abs(x)))


def _dot(a, b):
    return jnp.dot(a, b, preferred_element_type=F32)


def _dot_nt(a, b):
    return lax.dot_general(a, b, (((1,), (1,)), ((), ())), preferred_element_type=F32)


def _dot_tn(a, b):
    return lax.dot_general(a, b, (((0,), (0,)), ((), ())), preferred_element_type=F32)


def _each(f, *lists):
    return [f(*xs) for xs in zip(*lists)]


def _proj_kernel(x_ref, nw_ref, w_ref, wt_ref, *out_refs, prompt):
    x = x_ref[...]
    ms = jnp.mean(x * x, axis=-1, keepdims=True)
    hn = (x * lax.rsqrt(ms + EPS) * nw_ref[...]).astype(BF16)

    def mm(lo, hi):
        return _dot(hn, w_ref[:, lo:hi])

    if prompt:
        a3_ref, qt_ref, k32_ref, k16_ref, v32_ref, vt_ref, bg_ref, qkvc_ref, cm_ref = out_refs
        qt_ref[...] = (_dot_nt(wt_ref[0:W_B, :], hn) * (DK_B ** -0.5)).astype(BF16)
        vt_ref[...] = _dot_nt(wt_ref[W_B:2 * W_B, :], hn).astype(BF16)
        k = mm(C_K, C_V)
        k16_ref[...] = k.astype(BF16)
    else:
        a3_ref, q_ref, k32_ref, v32_ref, bg_ref, qkvc_ref, cm_ref = out_refs
        q_ref[...] = mm(C_Q, C_K) * (DK_B ** -0.5)
        k = mm(C_K, C_V)
    a3_ref[...] = mm(C_A3, C_Q)
    v = mm(C_V, C_BG)
    tm = x.shape[0]
    for h in range(H_B):
        k32_ref[pl.ds(h, tm, stride=H_B), :] = k[:, h * DV_B:(h + 1) * DV_B]
        v32_ref[pl.ds(h, tm, stride=H_B), :] = v[:, h * DV_B:(h + 1) * DV_B]
    bg_ref[...] = mm(C_BG, C_QKVC)
    qkvc_ref[...] = mm(C_QKVC, C_CG)
    cm_ref[...] = mm(C_CG, N_PROJ)


def _proj(x, nw, w, wt, tm, prompt):
    m, d = x.shape
    tok = lambda wd, dt: (pl.BlockSpec((tm, wd), lambda i: (i, 0)), jax.ShapeDtypeStruct((m, wd), dt))
    tr = lambda: (pl.BlockSpec((W_B, tm), lambda i: (0, i)), jax.ShapeDtypeStruct((W_B, m), BF16))
    hd = lambda: (pl.BlockSpec((tm * H_B, DV_B), lambda i: (i, 0)),
                  jax.ShapeDtypeStruct((m * H_B, DV_B), F32))
    if prompt:
        outs = [tok(3 * W_A, F32), tr(), hd(), tok(W_B, BF16), hd(), tr(),
                tok(W_B, F32), tok(QKV_C, F32), tok(W_CM, F32)]
    else:
        outs = [tok(3 * W_A, F32), tok(W_B, F32), hd(), hd(),
                tok(W_B, F32), tok(QKV_C, F32), tok(W_CM, F32)]
    return pl.pallas_call(
        functools.partial(_proj_kernel, prompt=prompt),
        grid=(m // tm,),
        in_specs=[pl.BlockSpec((tm, d), lambda i: (i, 0)),
                  pl.BlockSpec((1, d), lambda i: (0, 0)),
                  pl.BlockSpec((d, N_PROJ), lambda i: (0, 0)),
                  pl.BlockSpec((2 * W_B, d), lambda i: (0, 0))],
        out_specs=[o[0] for o in outs],
        out_shape=[o[1] for o in outs],
        compiler_params=_cparams(("parallel",)),
        name="proj",
    )(x, nw, w, wt)


def _outproj_kernel(ya_ref, yb_ref, yc_ref, h_ref, w_ref, fw_ref, o_ref, *, final):
    ycat = jnp.concatenate([ya_ref[...].astype(BF16), yb_ref[...].astype(BF16),
                            yc_ref[...].astype(BF16)], axis=-1)
    hnew = h_ref[...] + _dot(ycat, w_ref[...])
    if final:
        ms = jnp.mean(hnew * hnew, axis=-1, keepdims=True)
        hnew = hnew * lax.rsqrt(ms + EPS) * fw_ref[...]
    o_ref[...] = hnew


def _outproj(ya, yb, yc, h, w, fw, tm, final):
    m, d = h.shape
    row = lambda wd: pl.BlockSpec((tm, wd), lambda i: (i, 0))
    return pl.pallas_call(
        functools.partial(_outproj_kernel, final=final),
        grid=(m // tm,),
        in_specs=[row(W_A), row(W_B), row(W_C), row(d),
                  pl.BlockSpec((D_MIX, d), lambda i: (0, 0)),
                  pl.BlockSpec((1, d), lambda i: (0, 0))],
        out_specs=row(d),
        out_shape=jax.ShapeDtypeStruct((m, d), F32),
        compiler_params=_cparams(("parallel",)),
        name="outproj",
    )(ya, yb, yc, h, w, fw)


def _conva_tail(conv, a_gate, cb, lnw, lnb, pw, pb):
    y = conv + cb
    mu = jnp.mean(y, axis=-1, keepdims=True)
    yc = y - mu
    y = yc * lax.rsqrt(jnp.mean(yc * yc, axis=-1, keepdims=True) + EPS) * lnw + lnb
    y = _silu(y)
    y = _dot(y.astype(BF16), pw) + pb
    return y * _silu(a_gate)


CONV_PRE = 32
CONV_OFF = CONV_PRE - (CONV_A - 1)
CONV_RC = 64


def _conva_prompt_kernel(a3_ref, cw_ref, cb_ref, lnw_ref, lnb_ref, pw_ref, pb_ref,
                         ya_ref, tail_ref, ext_ref, *, tl):
    i = pl.program_id(1)

    @pl.when(i == 0)
    def _():
        ext_ref[0:CONV_PRE, :] = jnp.zeros((CONV_PRE, W_A), F32)

    u = a3_ref[:, 0:W_A] * _sigmoid(a3_ref[:, W_A:2 * W_A])
    ext_ref[CONV_PRE:CONV_PRE + tl, :] = u
    for r in range(tl // CONV_RC):
        base = r * CONV_RC + CONV_OFF
        acc = jnp.zeros((CONV_RC, W_A), F32)
        for j in range(CONV_A):
            acc = acc + cw_ref[j:j + 1, :] * ext_ref[base + j:base + j + CONV_RC, :]
        gate = a3_ref[r * CONV_RC:(r + 1) * CONV_RC, 2 * W_A:3 * W_A]
        y = _conva_tail(acc, gate, cb_ref[...], lnw_ref[...], lnb_ref[...], pw_ref[...], pb_ref[...])
        ya_ref[r * CONV_RC:(r + 1) * CONV_RC, :] = y.astype(BF16)
    last = ext_ref[tl:tl + CONV_PRE, :]
    tail_ref[...] = last
    ext_ref[0:CONV_PRE, :] = last


def _conva_prompt(a3, cw, cb, lnw, lnb, pw, pb, bsz, seq, tl):
    nl = seq // tl
    full = lambda shp: pl.BlockSpec(shp, lambda b, i: (0,) * len(shp))
    return pl.pallas_call(
        functools.partial(_conva_prompt_kernel, tl=tl),
        grid=(bsz, nl),
        in_specs=[pl.BlockSpec((tl, 3 * W_A), lambda b, i: (b * nl + i, 0)),
                  full((CONV_A, W_A)), full((1, W_A)), full((1, W_A)), full((1, W_A)),
                  full((W_A, W_A)), full((1, W_A))],
        out_specs=[pl.BlockSpec((tl, W_A), lambda b, i: (b * nl + i, 0)),
                   pl.BlockSpec((None, CONV_PRE, W_A), lambda b, i: (b, 0, 0))],
        out_shape=[jax.ShapeDtypeStruct((bsz * seq, W_A), BF16),
                   jax.ShapeDtypeStruct((bsz, CONV_PRE, W_A), F32)],
        scratch_shapes=[pltpu.VMEM((CONV_PRE + tl, W_A), F32)],
        compiler_params=_cparams(("parallel", "arbitrary")),
        name="conva_prompt",
    )(a3, cw, cb, lnw, lnb, pw, pb)


SEQ_GROUP = 8


def _conva_sample_kernel(a3_ref, st_ref, cw_ref, cb_ref, lnw_ref, lnb_ref, pw_ref, pb_ref,
                         ya_ref, u_ref, ext_ref):
    a3 = a3_ref[...]
    u = a3[:, :, 0:W_A] * _sigmoid(a3[:, :, W_A:2 * W_A])
    u_ref[...] = u
    ext_ref[:, 0:CONV_PRE, :] = st_ref[...]
    ext_ref[:, CONV_PRE:CONV_PRE + SROWS, :] = u
    acc = jnp.zeros((SEQ_GROUP, SROWS, W_A), F32)
    for j in range(CONV_A):
        acc = acc + cw_ref[j:j + 1, :] * ext_ref[:, CONV_OFF + j:CONV_OFF + j + SROWS, :]
    rows = SEQ_GROUP * SROWS
    y = _conva_tail(acc.reshape(rows, W_A), a3[:, :, 2 * W_A:3 * W_A].reshape(rows, W_A),
                    cb_ref[...], lnw_ref[...], lnb_ref[...], pw_ref[...], pb_ref[...])
    ya_ref[...] = y.reshape(SEQ_GROUP, SROWS, W_A)


def _conva_sample(a3, st, cw, cb, lnw, lnb, pw, pb):
    nb = a3.shape[0]
    full = lambda shp: pl.BlockSpec(shp, lambda g: (0,) * len(shp))
    seqs = lambda r, wd: pl.BlockSpec((SEQ_GROUP, r, wd), lambda g: (g, 0, 0))
    return pl.pallas_call(
        _conva_sample_kernel,
        grid=(nb // SEQ_GROUP,),
        in_specs=[seqs(SROWS, 3 * W_A), seqs(CONV_PRE, W_A),
                  full((CONV_A, W_A)), full((1, W_A)), full((1, W_A)), full((1, W_A)),
                  full((W_A, W_A)), full((1, W_A))],
        out_specs=[seqs(SROWS, W_A), seqs(SROWS, W_A)],
        out_shape=[jax.ShapeDtypeStruct((nb, SROWS, W_A), F32),
                   jax.ShapeDtypeStruct((nb, SROWS, W_A), F32)],
        scratch_shapes=[pltpu.VMEM((SEQ_GROUP, CONV_PRE + SROWS, W_A), F32)],
        compiler_params=_cparams(("parallel",)),
        name="conva_sample",
    )(a3, st, cw, cb, lnw, lnb, pw, pb)


def _lambda_full(lq1, lk1, lq2, lk2, lam_init):
    s1 = jnp.sum(lq1 * lk1, axis=-1, keepdims=True)
    s2 = jnp.sum(lq2 * lk2, axis=-1, keepdims=True)
    return jnp.exp(s1) - jnp.exp(s2) + lam_init


def _subln_gate(o, sub, bg, lam_init):
    y = o * lax.rsqrt(jnp.mean(o * o, axis=-1, keepdims=True) + EPS) * sub
    return (y * (1.0 - lam_init)) * _silu(bg)


def _attn_prompt_kernel(qt_ref, k_ref, vt_ref, bg_ref, sub_ref, lq1_ref, lk1_ref, lq2_ref, lk2_ref,
                        o_ref, m1_ref, l1_ref, a1_ref, m2_ref, l2_ref, a2_ref, *, lam_init, t):
    qi = pl.program_id(2)
    qt = qt_ref[...]
    feat = lax.broadcasted_iota(jnp.int32, qt.shape, 0)
    zero = jnp.zeros_like(qt)
    qs = (jnp.where(feat < DK_B, qt, zero), jnp.where(feat >= DK_B, qt, zero))
    stats = ((m1_ref, l1_ref, a1_ref), (m2_ref, l2_ref, a2_ref))
    for m_ref, l_ref, a_ref in stats:
        m_ref[...] = jnp.full(m_ref.shape, -jnp.inf, F32)
        l_ref[...] = jnp.zeros(l_ref.shape, F32)
        a_ref[...] = jnp.zeros(a_ref.shape, F32)

    def tile(j, causal):
        start = pl.multiple_of(j * t, t)
        k = k_ref[pl.ds(start, t), :]
        vt = vt_ref[:, pl.ds(start, t)]
        for qm, (m_ref, l_ref, a_ref) in zip(qs, stats):
            st = _dot(k, qm)
            if causal:
                kr = lax.broadcasted_iota(jnp.int32, st.shape, 0)
                qc = lax.broadcasted_iota(jnp.int32, st.shape, 1)
                st = jnp.where(kr <= qc, st, -jnp.inf)
            m_prev = m_ref[...]
            m_new = jnp.maximum(m_prev, jnp.max(st, axis=0, keepdims=True))
            alpha = jnp.exp(m_prev - m_new)
            pt = jnp.exp(st - m_new)
            l_ref[...] = alpha * l_ref[...] + jnp.sum(pt, axis=0, keepdims=True)
            a_ref[...] = alpha * a_ref[...] + _dot(vt, pt.astype(BF16))
            m_ref[...] = m_new

    def body(j, carry):
        tile(j, False)
        return carry

    lax.fori_loop(0, qi, body, 0)
    tile(qi, True)
    lam = _lambda_full(lq1_ref[...], lk1_ref[...], lq2_ref[...], lk2_ref[...], lam_init)
    ot = a1_ref[...] / l1_ref[...] - lam * (a2_ref[...] / l2_ref[...])
    o_ref[...] = _subln_gate(ot.T, sub_ref[...], bg_ref[...], lam_init).astype(BF16)


def _attn_prompt(qt, k, vt, bg, sub, lq1, lk1, lq2, lk2, lam_init, bsz, seq, t):
    n = seq // t
    full = lambda shp: pl.BlockSpec(shp, lambda b, h, qi: (0,) * len(shp))
    return pl.pallas_call(
        functools.partial(_attn_prompt_kernel, lam_init=lam_init, t=t),
        grid=(bsz, H_B, n),
        in_specs=[pl.BlockSpec((DV_B, t), lambda b, h, qi: (h, b * n + qi)),
                  pl.BlockSpec((seq, DV_B), lambda b, h, qi: (b, h)),
                  pl.BlockSpec((DV_B, seq), lambda b, h, qi: (h, b)),
                  pl.BlockSpec((t, DV_B), lambda b, h, qi: (b * n + qi, h)),
                  full((1, DV_B)), full((1, DK_B)), full((1, DK_B)), full((1, DK_B)),
                  full((1, DK_B))],
        out_specs=pl.BlockSpec((t, DV_B), lambda b, h, qi: (b * n + qi, h)),
        out_shape=jax.ShapeDtypeStruct((bsz * seq, W_B), BF16),
        scratch_shapes=[pltpu.VMEM((1, t), F32), pltpu.VMEM((1, t), F32), pltpu.VMEM((DV_B, t), F32),
                        pltpu.VMEM((1, t), F32), pltpu.VMEM((1, t), F32), pltpu.VMEM((DV_B, t), F32)],
        compiler_params=_cparams(("parallel", "parallel", "arbitrary")),
        name="attn_prompt",
    )(qt, k, vt, bg, sub, lq1, lk1, lq2, lk2)


PAGES_PER_STEP = 8
N_QROWS = 2 * SROWS


def _attn_sample_kernel(pt_ref, q_ref, *refs, lam_init, n_steps, npg):
    kp_refs = refs[:npg]
    vp_refs = refs[npg:2 * npg]
    (kn_ref, vn_ref, bg_ref, sub_ref, lq1_ref, lk1_ref, lq2_ref, lk2_ref, o_ref,
     qb_ref, m_ref, l_ref, acc_ref) = refs[2 * npg:]
    p = pl.program_id(1)
    heads = range(H_B)

    @pl.when(p == 0)
    def _():
        q = q_ref[...]
        lane = lax.broadcasted_iota(jnp.int32, (SROWS, DV_B), 1)
        for h in heads:
            qh = q[:, h * DV_B:(h + 1) * DV_B]
            qb_ref[h, 0:SROWS, :] = jnp.where(lane < DK_B, qh, 0.0)
            qb_ref[h, SROWS:N_QROWS, :] = jnp.where(lane >= DK_B, qh, 0.0)
        m_ref[...] = jnp.full(m_ref.shape, -jnp.inf, F32)
        l_ref[...] = jnp.zeros(l_ref.shape, F32)
        acc_ref[...] = jnp.zeros(acc_ref.shape, F32)

    def update_all(ss, vs):
        m_prev = [m_ref[h] for h in heads]
        m_new = [jnp.maximum(mp, jnp.max(s, axis=-1, keepdims=True)) for mp, s in zip(m_prev, ss)]
        alpha = [jnp.exp(mp - mn) for mp, mn in zip(m_prev, m_new)]
        pr = [jnp.exp(s - mn) for s, mn in zip(ss, m_new)]
        pv = [_dot(x.astype(BF16), v) for x, v in zip(pr, vs)]
        for h in heads:
            l_ref[h] = alpha[h] * l_ref[h] + jnp.sum(pr[h], axis=-1, keepdims=True)
            acc_ref[h] = alpha[h] * acc_ref[h] + pv[h]
            m_ref[h] = m_new[h]

    def head_rows(page_refs, h):
        return jnp.concatenate([r[pl.ds(h, PAGE, stride=H_B), :].astype(BF16) for r in page_refs], axis=0)

    @pl.when(p < n_steps)
    def _():
        ks = [head_rows(kp_refs, h) for h in heads]
        ss = [_dot_nt(qb_ref[h].astype(BF16), ks[h]) for h in heads]
        update_all(ss, [head_rows(vp_refs, h) for h in heads])

    @pl.when(p == n_steps)
    def _():
        ss, vs = [], []
        for h in heads:
            kn = kn_ref[pl.ds(h, SROWS, stride=H_B), :].astype(BF16)
            s = _dot_nt(qb_ref[h].astype(BF16), kn)
            tq = lax.broadcasted_iota(jnp.int32, s.shape, 0) % SROWS
            tk = lax.broadcasted_iota(jnp.int32, s.shape, 1)
            ss.append(jnp.where(tk <= tq, s, -jnp.inf))
            vs.append(vn_ref[pl.ds(h, SROWS, stride=H_B), :].astype(BF16))
        update_all(ss, vs)
        lam = _lambda_full(lq1_ref[...], lk1_ref[...], lq2_ref[...], lk2_ref[...], lam_init)
        outs = []
        for h in heads:
            an = acc_ref[h] / l_ref[h]
            o = an[0:SROWS] - lam * an[SROWS:N_QROWS]
            outs.append(_subln_gate(o, sub_ref[...], bg_ref[:, h * DV_B:(h + 1) * DV_B], lam_init))
        o_ref[...] = jnp.concatenate(outs, axis=-1)


def _attn_sample(page_table, q, cache_k, cache_v, kn, vn, bg, sub, lq1, lk1, lq2, lk2, lam_init,
                 layer, npg):
    nb, n_pages = page_table.shape
    n_steps = n_pages // npg
    seq = lambda b, p, pt: (b, 0)

    def page(i):
        return lambda b, p, pt: (layer, pt[b, jnp.minimum(p, n_steps - 1) * npg + i], 0, 0)

    full = lambda shp: pl.BlockSpec(shp, lambda b, p, pt: (0,) * len(shp))
    pages = [pl.BlockSpec((None, None, PAGE * H_B, DV_B), page(i)) for i in range(npg)]
    tok = pl.BlockSpec((SROWS, W_B), seq)
    new = pl.BlockSpec((SROWS * H_B, DV_B), seq)
    grid_spec = pltpu.PrefetchScalarGridSpec(
        num_scalar_prefetch=1,
        grid=(nb, n_steps + 1),
        in_specs=[tok] + pages + pages + [new, new, tok, full((1, DV_B)), full((1, DK_B)),
                                          full((1, DK_B)), full((1, DK_B)), full((1, DK_B))],
        out_specs=tok,
        scratch_shapes=[pltpu.VMEM((H_B, N_QROWS, DV_B), F32), pltpu.VMEM((H_B, N_QROWS, 1), F32),
                        pltpu.VMEM((H_B, N_QROWS, 1), F32), pltpu.VMEM((H_B, N_QROWS, DV_B), F32)],
    )
    return pl.pallas_call(
        functools.partial(_attn_sample_kernel, lam_init=lam_init, n_steps=n_steps, npg=npg),
        grid_spec=grid_spec,
        out_shape=jax.ShapeDtypeStruct((nb * SROWS, W_B), F32),
        compiler_params=_cparams(("parallel", "arbitrary")),
        name="attn_sample",
    )(page_table, q, *([cache_k] * npg), *([cache_v] * npg), kn, vn, bg, sub, lq1, lk1, lq2, lk2)


def _dot_hi(a, b):
    return jnp.dot(a, b, preferred_element_type=F32, precision=HI)


def _split(x):
    hi = x.astype(BF16)
    return hi, (x - hi.astype(F32)).astype(BF16)


def _dot3(a, b):
    ah, al = a
    bh, bl = b
    return _dot(ah, bh) + (_dot(ah, bl) + _dot(al, bh))


def _tri_inverse_many(a_list, c):
    row = lax.broadcasted_iota(jnp.int32, (c, c), 0)
    col = lax.broadcasted_iota(jnp.int32, (c, c), 1)
    eye = (row == col).astype(F32)
    nb = min(c, 16)
    ad = [jnp.where(row // nb == col // nb, a, 0.0) for a in a_list]
    t = [eye - x for x in ad]
    pw = ad
    size = 2
    while size < nb:
        ps = _each(_split, pw)
        pw = _each(_dot3, ps, ps)
        tp = _each(_dot3, _each(_split, t), _each(_split, pw))
        t = _each(lambda x, y: x + y, t, tp)
        size *= 2
    size = nb * 2
    while size <= c:
        off = jnp.logical_and(row // size == col // size, row // (size // 2) != col // (size // 2))
        ts = _each(_split, t)
        inner = _each(_dot3, [_split(jnp.where(off, a, 0.0)) for a in a_list], ts)
        outer = _each(_dot3, ts, _each(_split, inner))
        t = _each(lambda x, y: x - y, t, outer)
        size *= 2
    return t


def _delta_prep_many(items, c):
    row = lax.broadcasted_iota(jnp.int32, (c, c), 0)
    col = lax.broadcasted_iota(jnp.int32, (c, c), 1)
    decay = [jnp.exp(jnp.where(row >= col, it[3] - it[4], -jnp.inf)) for it in items]
    kb = [it[1].astype(BF16) for it in items]
    kk = _each(_dot_nt, kb, kb)
    a = [jnp.where(row > col, it[5] * x * d, 0.0) for it, x, d in zip(items, kk, decay)]
    t = _tri_inverse_many(a, c)
    eg = [jnp.exp(it[3]) for it in items]
    rhs = [_split(jnp.concatenate([it[2] * it[5], it[1] * (it[5] * e)], axis=-1))
           for it, e in zip(items, eg)]
    uw = _each(_dot3, _each(_split, t), rhs)
    qk = _each(_dot_nt, [it[0].astype(BF16) for it in items], kb)
    out = []
    for it, x, e, s, d in zip(items, uw, eg, qk, decay):
        q, k, _, gcol, _, _ = it
        glast = gcol[c - 1:c, :]
        wq = jnp.concatenate([x[:, DV_C:], q * e], axis=0).astype(BF16)
        kt = (k * jnp.exp(glast - gcol)).astype(BF16)
        out.append((x[:, :DV_C], wq, (s * d).astype(BF16), kt, jnp.exp(glast)))
    return out


def _delta_scan(u, wq, qk, kt, gl, s, c):
    r = _dot(wq, s.astype(BF16))
    vnb = (u - r[:c]).astype(BF16)
    return r[c:] + _dot(qk, vnb), s * gl + _dot_tn(kt, vnb)


def _l2norm(x):
    return x * lax.rsqrt(jnp.sum(x * x, axis=-1, keepdims=True) + EPS)


def _gdn_block(x, cm, alog, dtb, onw, states, c, nseg, chained, valid):
    rows = nseg * c
    beta = _sigmoid(cm[:, W_C:W_C + LANES])
    g = -jnp.exp(alog) * _softplus(cm[:, W_C + LANES:W_C + 2 * LANES] + dtb)
    if valid is not None:
        beta = jnp.where(valid, beta, 0.0)
        g = jnp.where(valid, g, 0.0)
        x = jnp.where(valid, x, 0.0)
    row = lax.broadcasted_iota(jnp.int32, (rows, rows), 0)
    col = lax.broadcasted_iota(jnp.int32, (rows, rows), 1)
    same = row // c == col // c
    gcs = _dot_hi(jnp.logical_and(same, row >= col).astype(F32), g)
    gcs_t = lax.dot_general(g, jnp.logical_and(same, row <= col).astype(F32),
                            (((0,), (0,)), ((), ())), preferred_element_type=F32,
                            precision=HI)
    qn = [_l2norm(x[:, h * DK_C:(h + 1) * DK_C]) * (DK_C ** -0.5) for h in range(H_C)]
    kn = [_l2norm(x[:, W_C + h * DK_C:W_C + (h + 1) * DK_C]) for h in range(H_C)]
    keys = [(i, h) for i in range(nseg) for h in range(H_C)]
    seg = lambda a, i: a[i * c:(i + 1) * c]
    items = [(seg(qn[h], i), seg(kn[h], i),
              seg(x, i)[:, 2 * W_C + h * DV_C:2 * W_C + (h + 1) * DV_C],
              seg(gcs, i)[:, h:h + 1], gcs_t[h:h + 1, i * c:(i + 1) * c],
              seg(beta, i)[:, h:h + 1]) for i, h in keys]
    preps = dict(zip(keys, _delta_prep_many(items, c)))
    s_cur = list(states)
    ys = []
    for i in range(nseg):
        outs = []
        for h in range(H_C):
            si = h if chained else i * H_C + h
            o, s_cur[si] = _delta_scan(*preps[i, h], s_cur[si], c)
            o = o * lax.rsqrt(jnp.mean(o * o, axis=-1, keepdims=True) + EPS) * onw
            outs.append(o * _silu(seg(cm, i)[:, h * DV_C:(h + 1) * DV_C]))
        ys.append(jnp.concatenate(outs, axis=-1))
    return jnp.concatenate(ys, axis=0), s_cur


CONVC_PRE = SUBLANES
CONVC_OFF = CONVC_PRE - (SHORT_CONV - 1)
GDN_CHUNKS = 4


def _gdn_prompt_kernel(x_ref, cm_ref, cw_ref, alog_ref, dtb_ref, onw_ref, y_ref, s_ref,
                       ext_ref, st_ref, *, nch):
    i = pl.program_id(1)
    rows = nch * CHUNK

    @pl.when(i == 0)
    def _():
        ext_ref[0:CONVC_PRE, :] = jnp.zeros((CONVC_PRE, QKV_C), F32)
        st_ref[...] = jnp.zeros(st_ref.shape, F32)

    ext_ref[CONVC_PRE:CONVC_PRE + rows, :] = x_ref[...]
    acc = jnp.zeros((rows, QKV_C), F32)
    for j in range(SHORT_CONV):
        acc = acc + cw_ref[j:j + 1, :] * ext_ref[CONVC_OFF + j:CONVC_OFF + j + rows, :]
    ext_ref[0:CONVC_PRE, :] = ext_ref[rows:rows + CONVC_PRE, :]
    y, s_new = _gdn_block(_silu(acc), cm_ref[...], alog_ref[...], dtb_ref[...], onw_ref[...],
                          [st_ref[h] for h in range(H_C)], CHUNK, nch, True, None)
    y_ref[...] = y.astype(BF16)
    for h in range(H_C):
        st_ref[h] = s_new[h]
    s_ref[...] = st_ref[...]


def _gdn_prompt(x, cm, cw, alog, dtb, onw, bsz, seq, nch):
    rows = nch * CHUNK
    nc = seq // rows
    full = lambda shp: pl.BlockSpec(shp, lambda b, i: (0,) * len(shp))
    tok = lambda wd: pl.BlockSpec((rows, wd), lambda b, i: (b * nc + i, 0))
    return pl.pallas_call(
        functools.partial(_gdn_prompt_kernel, nch=nch),
        grid=(bsz, nc),
        in_specs=[tok(QKV_C), tok(W_CM), full((SHORT_CONV, QKV_C)), full((1, LANES)),
                  full((1, LANES)), full((1, DV_C))],
        out_specs=[tok(W_C), pl.BlockSpec((None, H_C, DK_C, DV_C), lambda b, i: (b, 0, 0, 0))],
        out_shape=[jax.ShapeDtypeStruct((bsz * seq, W_C), BF16),
                   jax.ShapeDtypeStruct((bsz, H_C, DK_C, DV_C), F32)],
        scratch_shapes=[pltpu.VMEM((CONVC_PRE + rows, QKV_C), F32),
                        pltpu.VMEM((H_C, DK_C, DV_C), F32)],
        compiler_params=_cparams(("parallel", "arbitrary")),
        name="gdn_prompt",
    )(x, cm, cw, alog, dtb, onw)


def _gdn_sample_kernel(x_ref, st_ref, cm_ref, s0_ref, cw_ref, alog_ref, dtb_ref, onw_ref,
                       y_ref, s_ref, *, n_real):
    accs = []
    for g in range(SEQ_GROUP):
        ext = jnp.concatenate([st_ref[g], x_ref[g * SROWS:(g + 1) * SROWS, :]], axis=0)
        acc = jnp.zeros((SROWS, QKV_C), F32)
        for j in range(SHORT_CONV):
            acc = acc + cw_ref[j:j + 1, :] * ext[CONVC_OFF + j:CONVC_OFF + j + SROWS, :]
        accs.append(acc)
    rows = SEQ_GROUP * SROWS
    valid = lax.broadcasted_iota(jnp.int32, (rows, 1), 0) % SROWS < n_real
    states = [s0_ref[g, h] for g in range(SEQ_GROUP) for h in range(H_C)]
    y, s_new = _gdn_block(_silu(jnp.concatenate(accs, axis=0)), cm_ref[...], alog_ref[...],
                          dtb_ref[...], onw_ref[...], states, SROWS, SEQ_GROUP, False, valid)
    y_ref[...] = y
    for g in range(SEQ_GROUP):
        for h in range(H_C):
            s_ref[g, h] = s_new[g * H_C + h]


def _gdn_sample(x, st, cm, s0, cw, alog, dtb, onw, n_real):
    nb = s0.shape[0]
    rows = SEQ_GROUP * SROWS
    full = lambda shp: pl.BlockSpec(shp, lambda g: (0,) * len(shp))
    tok = lambda wd: pl.BlockSpec((rows, wd), lambda g: (g, 0))
    state = pl.BlockSpec((SEQ_GROUP, H_C, DK_C, DV_C), lambda g: (g, 0, 0, 0))
    return pl.pallas_call(
        functools.partial(_gdn_sample_kernel, n_real=n_real),
        grid=(nb // SEQ_GROUP,),
        in_specs=[tok(QKV_C), pl.BlockSpec((SEQ_GROUP, CONVC_PRE, QKV_C), lambda g: (g, 0, 0)),
                  tok(W_CM), state, full((SHORT_CONV, QKV_C)), full((1, LANES)), full((1, LANES)),
                  full((1, DV_C))],
        out_specs=[tok(W_C), state],
        out_shape=[jax.ShapeDtypeStruct((nb * SROWS, W_C), F32),
                   jax.ShapeDtypeStruct((nb, H_C, DK_C, DV_C), F32)],
        compiler_params=_cparams(("parallel",)),
        name="gdn_sample",
    )(x, st, cm, s0, cw, alog, dtb, onw)


def _rearrange_w_in(w_in):
    depth, d, _ = w_in.shape
    n_main = 3 * W_A + 4 * W_B + QKV_C
    beta = w_in[:, :, n_main:n_main + H_C]
    alpha = w_in[:, :, n_main + H_C:n_main + 2 * H_C]
    gate = w_in[:, :, n_main + 2 * H_C:]
    zpad = jnp.zeros((depth, d, LANES - H_C), w_in.dtype)
    w = jnp.concatenate([w_in[:, :, :n_main], gate, beta, zpad, alpha, zpad], axis=-1).astype(BF16)
    wt = jnp.concatenate([w_in[:, :, C_Q:C_K], w_in[:, :, C_V:C_BG]], axis=-1)
    return w, jnp.swapaxes(wt, 1, 2).astype(BF16)


def _pad_lanes(x):
    return jnp.pad(x, ((0, 0), (0, LANES - x.shape[-1])))


def kernel(x_prompt, x_sample, cache_k, cache_v, state_conv_a, state_conv_c, state_delta, page_table,
           norm_w, w_in, conv_a_w, conv_a_b, ln_a_w, ln_a_b, pw_a_w, pw_a_b,
           lambda_q1, lambda_k1, lambda_q2, lambda_k2, subln_w, conv_c_w, a_log, dt_bias,
           onorm_c_w, w_out, final_norm_w):
    bsz, seq, d = x_prompt.shape
    nb, ns, _ = x_sample.shape
    depth = w_in.shape[0]
    n_pool = cache_k.shape[1]
    n_pages = page_table.shape[1]
    t_attn = 512 if seq % 512 == 0 else seq
    tl_conv = t_attn
    tm_p = 256
    nch = GDN_CHUNKS if seq % (GDN_CHUNKS * CHUNK) == 0 else 1
    npg = PAGES_PER_STEP if n_pages % PAGES_PER_STEP == 0 else 1
    assert SHORT_CONV - 1 <= ns <= SROWS, "sample tokens must fit one 8-row tile"
    assert nb % SEQ_GROUP == 0 and seq % CHUNK == 0 and seq % t_attn == 0 and (bsz * seq) % tm_p == 0
    assert tl_conv % CONV_RC == 0 and tl_conv >= CONV_PRE

    wp, wt = _rearrange_w_in(w_in)
    wo = w_out.astype(BF16)
    pw = pw_a_w.astype(BF16)
    ck = cache_k.reshape(depth, n_pool, PAGE * H_B, DV_B)
    cv = cache_v.reshape(depth, n_pool, PAGE * H_B, DV_B)
    st_a = jnp.pad(state_conv_a, ((0, 0), (0, 0), (CONV_PRE - (CONV_A - 1), 0), (0, 0)))
    st_c = jnp.pad(state_conv_c, ((0, 0), (0, 0), (CONVC_PRE - (SHORT_CONV - 1), 0), (0, 0)))
    alog = _pad_lanes(a_log)
    dtb = _pad_lanes(dt_bias)
    fw = final_norm_w.reshape(1, d)

    hp = x_prompt.reshape(bsz * seq, d)
    hs = jnp.pad(x_sample, ((0, 0), (0, SROWS - ns), (0, 0))).reshape(nb * SROWS, d)
    outs_p = [[] for _ in range(5)]
    outs_s = [[] for _ in range(5)]
    for l in range(depth):
        lam_init = 0.8 - 0.6 * math.exp(-0.3 * l)
        final = l == depth - 1
        row = lambda a: a[l].reshape(1, -1)
        lams = (row(lambda_q1), row(lambda_k1), row(lambda_q2), row(lambda_k2))

        a3, qt, k32, k16, v32, vt, bg, qkvc, cm = _proj(hp, row(norm_w), wp[l], wt[l], tm_p, True)
        ya, tail = _conva_prompt(a3, conv_a_w[l], row(conv_a_b), row(ln_a_w), row(ln_a_b), pw[l],
                                 row(pw_a_b), bsz, seq, tl_conv)
        yb = _attn_prompt(qt, k16, vt, bg, row(subln_w), *lams, lam_init, bsz, seq, t_attn)
        yc, s_new = _gdn_prompt(qkvc, cm, conv_c_w[l], alog[l:l + 1], dtb[l:l + 1], row(onorm_c_w),
                                bsz, seq, nch)
        hp = _outproj(ya, yb, yc, hp, wo[l], fw, tm_p, final)
        outs_p[0].append(k32.reshape(bsz, seq, H_B, 2 * DK_B))
        outs_p[1].append(v32.reshape(bsz, seq, H_B, DV_B))
        outs_p[2].append(tail[:, CONV_PRE - (CONV_A - 1):])
        outs_p[3].append(qkvc.reshape(bsz, seq, QKV_C)[:, seq - (SHORT_CONV - 1):])
        outs_p[4].append(s_new)

        a3, q, k32, v32, bg, qkvc, cm = _proj(hs, row(norm_w), wp[l], wt[l], nb * SROWS, False)
        ya, u = _conva_sample(a3.reshape(nb, SROWS, 3 * W_A), st_a[l], conv_a_w[l], row(conv_a_b),
                              row(ln_a_w), row(ln_a_b), pw[l], row(pw_a_b))
        yb = _attn_sample(page_table, q, ck, cv, k32, v32, bg, row(subln_w), *lams, lam_init, l, npg)
        yc, s_new = _gdn_sample(qkvc, st_c[l], cm, state_delta[l], conv_c_w[l], alog[l:l + 1],
                                dtb[l:l + 1], row(onorm_c_w), ns)
        hs = _outproj(ya.reshape(nb * SROWS, W_A), yb, yc, hs, wo[l], fw, nb * SROWS, final)
        outs_s[0].append(k32.reshape(nb, SROWS, H_B, 2 * DK_B)[:, :ns])
        outs_s[1].append(v32.reshape(nb, SROWS, H_B, DV_B)[:, :ns])
        outs_s[2].append(jnp.concatenate([state_conv_a[l], u[:, :ns]], axis=1)[:, -(CONV_A - 1):])
        outs_s[3].append(jnp.concatenate([state_conv_c[l], qkvc.reshape(nb, SROWS, QKV_C)[:, :ns]],
                                         axis=1)[:, -(SHORT_CONV - 1):])
        outs_s[4].append(s_new)

    y_prompt = hp.reshape(bsz, seq, d)
    y_sample = hs.reshape(nb, SROWS, d)[:, :ns]
    stack = lambda xs: jnp.stack(xs)
    return (y_prompt, y_sample,
            stack(outs_p[0]), stack(outs_p[1]), stack(outs_p[2]), stack(outs_p[3]),
            stack(outs_p[4]).astype(state_delta.dtype),
            stack(outs_s[0]), stack(outs_s[1]), stack(outs_s[2]), stack(outs_s[3]),
            stack(outs_s[4]).astype(state_delta.dtype))
```

```python
import functools
import math

import jax
import jax.numpy as jnp
from jax import lax
from jax.experimental import pallas as pl
from jax.experimental.pallas import tpu as pltpu

F32 = jnp.float32
BF16 = jnp.bfloat16
HI = lax.Precision.HIGHEST

EPS = 1e-6
LOG2E = math.log2(math.e)
W_A = 256
CONV_A = 31
H_B = 4
DK_B = 64
DV_B = 128
W_B = H_B * DV_B
H_C = 4
DK_C = 64
DV_C = 64
W_C = H_C * DV_C
QKV_C = H_C * (2 * DK_C + DV_C)
SHORT_CONV = 4
CHUNK = 64
PAGE = 128
D_MIX = W_A + W_B + W_C

LANES = 128
SUBLANES = 8
SROWS = SUBLANES
VMEM_LIMIT = 48 * 1024 * 1024

C_A3 = 0
C_Q = C_A3 + 3 * W_A
C_K = C_Q + W_B
C_V = C_K + W_B
C_BG = C_V + W_B
C_QKVC = C_BG + W_B
C_CG = C_QKVC + QKV_C
W_CM = W_C + 2 * LANES


def _cparams(sem):
    return pltpu.CompilerParams(dimension_semantics=sem, vmem_limit_bytes=VMEM_LIMIT)


def _sigmoid(x):
    return 1.0 / (1.0 + jnp.exp(-x))


def _silu(x):
    return x * _sigmoid(x)


def _softplus(x):
    return jnp.maximum(x, 0.0) + jnp.log1p(jnp.exp(-jnp.abs(x)))


def _dot(a, b):
    return jnp.dot(a, b, preferred_element_type=F32)


def _dot_nt(a, b):
    return lax.dot_general(a, b, (((1,), (1,)), ((), ())), preferred_element_type=F32)


def _dot_tn(a, b):
    return lax.dot_general(a, b, (((0,), (0,)), ((), ())), preferred_element_type=F32)


def _each(f, *lists):
    return [f(*xs) for xs in zip(*lists)]


def _proj_kernel(x_ref, nw_ref, w_ref, wtail_ref, wt_ref, *out_refs, prompt):
    x = x_ref[...]
    ms = jnp.mean(x * x, axis=-1, keepdims=True)
    hn = (x * lax.rsqrt(ms + EPS) * nw_ref[...]).astype(BF16)

    def mm(lo, hi):
        return _dot(hn, w_ref[:, lo:hi])

    if prompt:
        a3_ref, qt_ref, k32_ref, k16_ref, v32_ref, vt_ref, bg_ref, qkvc_ref, cm_ref = out_refs
        qt_ref[...] = (_dot_nt(wt_ref[0:W_B, :], hn) * (LOG2E * DK_B ** -0.5)).astype(BF16)
        vt = _dot_nt(wt_ref[W_B:2 * W_B, :], hn)
        vt_ref[...] = vt.astype(BF16)
        v = vt.T
        k = mm(C_K, C_V)
        k16_ref[...] = k.astype(BF16)
    else:
        a3_ref, q_ref, k32_ref, v32_ref, bg_ref, qkvc_ref, cm_ref = out_refs
        q_ref[...] = mm(C_Q, C_K) * (DK_B ** -0.5)
        k = mm(C_K, C_V)
        v = mm(C_V, C_BG)
    a3_ref[...] = mm(C_A3, C_Q)
    tm = x.shape[0]
    for h in range(H_B):
        k32_ref[pl.ds(h, tm, stride=H_B), :] = k[:, h * DV_B:(h + 1) * DV_B]
        v32_ref[pl.ds(h, tm, stride=H_B), :] = v[:, h * DV_B:(h + 1) * DV_B]
    bg_ref[...] = mm(C_BG, C_QKVC)
    qkvc_ref[...] = mm(C_QKVC, C_CG)
    cm_ref[...] = _dot(hn, wtail_ref[...])


def _proj(x, nw, w, wtail, wt, layer, tm, prompt):
    m, d = x.shape
    n_in = w.shape[-1]
    per_layer = lambda r, c: pl.BlockSpec((None, r, c), lambda i: (layer, 0, 0))
    tok = lambda wd, dt: (pl.BlockSpec((tm, wd), lambda i: (i, 0)), jax.ShapeDtypeStruct((m, wd), dt))
    tr = lambda: (pl.BlockSpec((W_B, tm), lambda i: (0, i)), jax.ShapeDtypeStruct((W_B, m), BF16))
    hd = lambda: (pl.BlockSpec((tm * H_B, DV_B), lambda i: (i, 0)),
                  jax.ShapeDtypeStruct((m * H_B, DV_B), F32))
    if prompt:
        outs = [tok(3 * W_A, F32), tr(), hd(), tok(W_B, BF16), hd(), tr(),
                tok(W_B, F32), tok(QKV_C, F32), tok(W_CM, F32)]
    else:
        outs = [tok(3 * W_A, F32), tok(W_B, F32), hd(), hd(),
                tok(W_B, F32), tok(QKV_C, F32), tok(W_CM, F32)]
    return pl.pallas_call(
        functools.partial(_proj_kernel, prompt=prompt),
        grid=(m // tm,),
        in_specs=[pl.BlockSpec((tm, d), lambda i: (i, 0)),
                  pl.BlockSpec((1, d), lambda i: (0, 0)),
                  per_layer(d, n_in), per_layer(d, W_CM), per_layer(2 * W_B, d)],
        out_specs=[o[0] for o in outs],
        out_shape=[o[1] for o in outs],
        compiler_params=_cparams(("parallel",)),
        name="proj",
    )(x, nw, w, wtail, wt)


def _outproj_kernel(ya_ref, yb_ref, yc_ref, h_ref, w_ref, fw_ref, o_ref, *, final):
    ycat = jnp.concatenate([ya_ref[...].astype(BF16), yb_ref[...].astype(BF16),
                            yc_ref[...].astype(BF16)], axis=-1)
    hnew = h_ref[...] + _dot(ycat, w_ref[...])
    if final:
        ms = jnp.mean(hnew * hnew, axis=-1, keepdims=True)
        hnew = hnew * lax.rsqrt(ms + EPS) * fw_ref[...]
    o_ref[...] = hnew


def _outproj(ya, yb, yc, h, w, fw, tm, final):
    m, d = h.shape
    row = lambda wd: pl.BlockSpec((tm, wd), lambda i: (i, 0))
    return pl.pallas_call(
        functools.partial(_outproj_kernel, final=final),
        grid=(m // tm,),
        in_specs=[row(W_A), row(W_B), row(W_C), row(d),
                  pl.BlockSpec((D_MIX, d), lambda i: (0, 0)),
                  pl.BlockSpec((1, d), lambda i: (0, 0))],
        out_specs=row(d),
        out_shape=jax.ShapeDtypeStruct((m, d), F32),
        compiler_params=_cparams(("parallel",)),
        name="outproj",
    )(ya, yb, yc, h, w, fw)


def _conva_tail(conv, a_gate, cb, lnw, lnb, pw, pb):
    y = conv + cb
    mu = jnp.mean(y, axis=-1, keepdims=True)
    yc = y - mu
    y = yc * lax.rsqrt(jnp.mean(yc * yc, axis=-1, keepdims=True) + EPS) * lnw + lnb
    y = _silu(y)
    y = _dot(y.astype(BF16), pw) + pb
    return y * _silu(a_gate)


CONV_PRE = 32
CONV_OFF = CONV_PRE - (CONV_A - 1)
CONV_RC = 64


def _conva_prompt_kernel(a3_ref, cw_ref, cb_ref, lnw_ref, lnb_ref, pw_ref, pb_ref,
                         ya_ref, tail_ref, ext_ref, *, tl):
    i = pl.program_id(1)

    @pl.when(i == 0)
    def _():
        ext_ref[0, 0:CONV_PRE, :] = jnp.zeros((CONV_PRE, W_A), F32)

    u = a3_ref[:, 0:W_A] * _sigmoid(a3_ref[:, W_A:2 * W_A])
    ext_ref[0, CONV_PRE:CONV_PRE + tl, :] = u
    n_shift = CONV_PRE + tl - SUBLANES
    for s in range(1, SUBLANES):
        ext_ref[s, 0:n_shift, :] = ext_ref[0, s:s + n_shift, :]
    for r in range(tl // CONV_RC):
        acc = jnp.zeros((CONV_RC, W_A), F32)
        for j in range(CONV_A):
            off = r * CONV_RC + CONV_OFF + j
            s = off % SUBLANES
            acc = acc + cw_ref[j:j + 1, :] * ext_ref[s, off - s:off - s + CONV_RC, :]
        gate = a3_ref[r * CONV_RC:(r + 1) * CONV_RC, 2 * W_A:3 * W_A]
        y = _conva_tail(acc, gate, cb_ref[...], lnw_ref[...], lnb_ref[...], pw_ref[...], pb_ref[...])
        ya_ref[r * CONV_RC:(r + 1) * CONV_RC, :] = y.astype(BF16)
    last = ext_ref[0, tl:tl + CONV_PRE, :]
    tail_ref[...] = last
    ext_ref[0, 0:CONV_PRE, :] = last


def _conva_prompt(a3, cw, cb, lnw, lnb, pw, pb, bsz, seq, tl):
    nl = seq // tl
    full = lambda shp: pl.BlockSpec(shp, lambda b, i: (0,) * len(shp))
    return pl.pallas_call(
        functools.partial(_conva_prompt_kernel, tl=tl),
        grid=(bsz, nl),
        in_specs=[pl.BlockSpec((tl, 3 * W_A), lambda b, i: (b * nl + i, 0)),
                  full((CONV_A, W_A)), full((1, W_A)), full((1, W_A)), full((1, W_A)),
                  full((W_A, W_A)), full((1, W_A))],
        out_specs=[pl.BlockSpec((tl, W_A), lambda b, i: (b * nl + i, 0)),
                   pl.BlockSpec((None, CONV_PRE, W_A), lambda b, i: (b, 0, 0))],
        out_shape=[jax.ShapeDtypeStruct((bsz * seq, W_A), BF16),
                   jax.ShapeDtypeStruct((bsz, CONV_PRE, W_A), F32)],
        scratch_shapes=[pltpu.VMEM((SUBLANES, CONV_PRE + tl, W_A), F32)],
        compiler_params=_cparams(("parallel", "arbitrary")),
        name="conva_prompt",
    )(a3, cw, cb, lnw, lnb, pw, pb)


SEQ_GROUP = 8


def _conva_sample_kernel(a3_ref, st_ref, cw_ref, cb_ref, lnw_ref, lnb_ref, pw_ref, pb_ref,
                         ya_ref, u_ref, ext_ref):
    a3 = a3_ref[...]
    u = a3[:, :, 0:W_A] * _sigmoid(a3[:, :, W_A:2 * W_A])
    u_ref[...] = u
    ext_ref[:, 0:CONV_PRE, :] = st_ref[...]
    ext_ref[:, CONV_PRE:CONV_PRE + SROWS, :] = u
    acc = jnp.zeros((SEQ_GROUP, SROWS, W_A), F32)
    for j in range(CONV_A):
        acc = acc + cw_ref[j:j + 1, :] * ext_ref[:, CONV_OFF + j:CONV_OFF + j + SROWS, :]
    rows = SEQ_GROUP * SROWS
    y = _conva_tail(acc.reshape(rows, W_A), a3[:, :, 2 * W_A:3 * W_A].reshape(rows, W_A),
                    cb_ref[...], lnw_ref[...], lnb_ref[...], pw_ref[...], pb_ref[...])
    ya_ref[...] = y.reshape(SEQ_GROUP, SROWS, W_A)


def _conva_sample(a3, st, cw, cb, lnw, lnb, pw, pb):
    nb = a3.shape[0]
    full = lambda shp: pl.BlockSpec(shp, lambda g: (0,) * len(shp))
    seqs = lambda r, wd: pl.BlockSpec((SEQ_GROUP, r, wd), lambda g: (g, 0, 0))
    return pl.pallas_call(
        _conva_sample_kernel,
        grid=(nb // SEQ_GROUP,),
        in_specs=[seqs(SROWS, 3 * W_A), seqs(CONV_PRE, W_A),
                  full((CONV_A, W_A)), full((1, W_A)), full((1, W_A)), full((1, W_A)),
                  full((W_A, W_A)), full((1, W_A))],
        out_specs=[seqs(SROWS, W_A), seqs(SROWS, W_A)],
        out_shape=[jax.ShapeDtypeStruct((nb, SROWS, W_A), F32),
                   jax.ShapeDtypeStruct((nb, SROWS, W_A), F32)],
        scratch_shapes=[pltpu.VMEM((SEQ_GROUP, CONV_PRE + SROWS, W_A), F32)],
        compiler_params=_cparams(("parallel",)),
        name="conva_sample",
    )(a3, st, cw, cb, lnw, lnb, pw, pb)


def _lambda_full(lq1, lk1, lq2, lk2, lam_init):
    s1 = jnp.sum(lq1 * lk1, axis=-1, keepdims=True)
    s2 = jnp.sum(lq2 * lk2, axis=-1, keepdims=True)
    return jnp.exp(s1) - jnp.exp(s2) + lam_init


def _subln_gate(o, sub, bg, lam_init):
    y = o * lax.rsqrt(jnp.mean(o * o, axis=-1, keepdims=True) + EPS) * sub
    return (y * (1.0 - lam_init)) * _silu(bg)


ATTN_TILE = 1024
ATTN_STRIP = 256


def _attn_prompt_kernel(qt_ref, k_ref, vt_ref, bg_ref, sub_ref, lq1_ref, lk1_ref, lq2_ref, lk2_ref,
                        o_ref, m1_ref, l1_ref, a1_ref, m2_ref, l2_ref, a2_ref, *, lam_init, t):
    qi = pl.program_id(2)
    qt = qt_ref[...]
    feat = lax.broadcasted_iota(jnp.int32, qt.shape, 0)
    zero = jnp.zeros_like(qt)
    qs = (jnp.where(feat < DK_B, qt, zero), jnp.where(feat >= DK_B, qt, zero))
    stats = ((m1_ref, l1_ref, a1_ref), (m2_ref, l2_ref, a2_ref))
    for m_ref, l_ref, a_ref in stats:
        m_ref[...] = jnp.full(m_ref.shape, -jnp.inf, F32)
        l_ref[...] = jnp.zeros(l_ref.shape, F32)
        a_ref[...] = jnp.zeros(a_ref.shape, F32)

    sw = min(ATTN_STRIP, t)
    units = [(mp, c) for c in range(t // sw) for mp in (0, 1)]

    def tile(j, causal):
        start = pl.multiple_of(j * t, t)
        k = k_ref[pl.ds(start, t), :]
        vt = vt_ref[:, pl.ds(start, t)]
        nkeys = lambda c: (c + 1) * sw if causal else t

        def scores(u):
            mp, c = u
            return _dot(k[:nkeys(c)], qs[mp][:, c * sw:(c + 1) * sw])

        def softmax(u, st):
            mp, c = u
            m_ref, l_ref, _ = stats[mp]
            cols = slice(c * sw, (c + 1) * sw)
            if causal:
                kr = lax.broadcasted_iota(jnp.int32, st.shape, 0)
                qc = lax.broadcasted_iota(jnp.int32, st.shape, 1) + c * sw
                st = jnp.where(kr <= qc, st, -jnp.inf)
            m_prev = m_ref[:, cols]
            m_new = jnp.maximum(m_prev, jnp.max(st, axis=0, keepdims=True))
            alpha = jnp.exp2(m_prev - m_new)
            pt = jnp.exp2(st - m_new)
            l_ref[:, cols] = alpha * l_ref[:, cols] + jnp.sum(pt, axis=0, keepdims=True)
            m_ref[:, cols] = m_new
            return pt.astype(BF16), alpha

        def weighted_values(u, pt, alpha):
            mp, c = u
            a_ref = stats[mp][2]
            cols = slice(c * sw, (c + 1) * sw)
            a_ref[:, cols] = alpha * a_ref[:, cols] + _dot(vt[:, :nkeys(c)], pt)

        n = len(units)
        sts, pts = {}, {}
        for i in range(n + 2):
            if i < n:
                sts[i] = scores(units[i])
            if 0 <= i - 1 < n:
                pts[i - 1] = softmax(units[i - 1], sts.pop(i - 1))
            if 0 <= i - 2 < n:
                weighted_values(units[i - 2], *pts.pop(i - 2))

    def body(j, carry):
        tile(j, False)
        return carry

    lax.fori_loop(0, qi, body, 0)
    tile(qi, True)
    lam = _lambda_full(lq1_ref[...], lk1_ref[...], lq2_ref[...], lk2_ref[...], lam_init)
    ot = a1_ref[...] / l1_ref[...] - lam * (a2_ref[...] / l2_ref[...])
    o_ref[...] = _subln_gate(ot.T, sub_ref[...], bg_ref[...], lam_init).astype(BF16)


def _attn_prompt(qt, k, vt, bg, sub, lq1, lk1, lq2, lk2, lam_init, bsz, seq, t):
    n = seq // t
    full = lambda shp: pl.BlockSpec(shp, lambda b, h, qi: (0,) * len(shp))
    return pl.pallas_call(
        functools.partial(_attn_prompt_kernel, lam_init=lam_init, t=t),
        grid=(bsz, H_B, n),
        in_specs=[pl.BlockSpec((DV_B, t), lambda b, h, qi: (h, b * n + qi)),
                  pl.BlockSpec((seq, DV_B), lambda b, h, qi: (b, h)),
                  pl.BlockSpec((DV_B, seq), lambda b, h, qi: (h, b)),
                  pl.BlockSpec((t, DV_B), lambda b, h, qi: (b * n + qi, h)),
                  full((1, DV_B)), full((1, DK_B)), full((1, DK_B)), full((1, DK_B)),
                  full((1, DK_B))],
        out_specs=pl.BlockSpec((t, DV_B), lambda b, h, qi: (b * n + qi, h)),
        out_shape=jax.ShapeDtypeStruct((bsz * seq, W_B), BF16),
        scratch_shapes=[pltpu.VMEM((1, t), F32), pltpu.VMEM((1, t), F32), pltpu.VMEM((DV_B, t), F32),
                        pltpu.VMEM((1, t), F32), pltpu.VMEM((1, t), F32), pltpu.VMEM((DV_B, t), F32)],
        compiler_params=_cparams(("parallel", "parallel", "arbitrary")),
        name="attn_prompt",
    )(qt, k, vt, bg, sub, lq1, lk1, lq2, lk2)


PAGES_PER_STEP = 16
N_QROWS = 2 * SROWS


def _attn_sample_kernel(pt_ref, q_ref, *refs, lam_init, n_steps, npg):
    kp_refs = refs[:npg]
    vp_refs = refs[npg:2 * npg]
    (kn_ref, vn_ref, bg_ref, sub_ref, lq1_ref, lk1_ref, lq2_ref, lk2_ref, o_ref,
     qb_ref, m_ref, l_ref, acc_ref) = refs[2 * npg:]
    p = pl.program_id(1)
    heads = range(H_B)

    @pl.when(p == 0)
    def _():
        q = q_ref[...]
        lane = lax.broadcasted_iota(jnp.int32, (SROWS, DV_B), 1)
        for h in heads:
            qh = q[:, h * DV_B:(h + 1) * DV_B]
            qb_ref[h, 0:SROWS, :] = jnp.where(lane < DK_B, qh, 0.0)
            qb_ref[h, SROWS:N_QROWS, :] = jnp.where(lane >= DK_B, qh, 0.0)
        m_ref[...] = jnp.full(m_ref.shape, -jnp.inf, F32)
        l_ref[...] = jnp.zeros(l_ref.shape, F32)
        acc_ref[...] = jnp.zeros(acc_ref.shape, F32)

    def update_all(ss, vs):
        m_prev = [m_ref[h] for h in heads]
        m_new = [jnp.maximum(mp, jnp.max(s, axis=-1, keepdims=True)) for mp, s in zip(m_prev, ss)]
        alpha = [jnp.exp(mp - mn) for mp, mn in zip(m_prev, m_new)]
        pr = [jnp.exp(s - mn) for s, mn in zip(ss, m_new)]
        pv = [_dot(x.astype(BF16), v) for x, v in zip(pr, vs)]
        for h in heads:
            l_ref[h] = alpha[h] * l_ref[h] + jnp.sum(pr[h], axis=-1, keepdims=True)
            acc_ref[h] = alpha[h] * acc_ref[h] + pv[h]
            m_ref[h] = m_new[h]

    def head_rows(page_refs, h):
        return jnp.concatenate([r[pl.ds(h, PAGE, stride=H_B), :].astype(BF16) for r in page_refs], axis=0)

    @pl.when(p < n_steps)
    def _():
        ks = [head_rows(kp_refs, h) for h in heads]
        ss = [_dot_nt(qb_ref[h].astype(BF16), ks[h]) for h in heads]
        update_all(ss, [head_rows(vp_refs, h) for h in heads])

    @pl.when(p == n_steps)
    def _():
        ss, vs = [], []
        for h in heads:
            kn = kn_ref[pl.ds(h, SROWS, stride=H_B), :].astype(BF16)
            s = _dot_nt(qb_ref[h].astype(BF16), kn)
            tq = lax.broadcasted_iota(jnp.int32, s.shape, 0) % SROWS
            tk = lax.broadcasted_iota(jnp.int32, s.shape, 1)
            ss.append(jnp.where(tk <= tq, s, -jnp.inf))
            vs.append(vn_ref[pl.ds(h, SROWS, stride=H_B), :].astype(BF16))
        update_all(ss, vs)
        lam = _lambda_full(lq1_ref[...], lk1_ref[...], lq2_ref[...], lk2_ref[...], lam_init)
        outs = []
        for h in heads:
            an = acc_ref[h] / l_ref[h]
            o = an[0:SROWS] - lam * an[SROWS:N_QROWS]
            outs.append(_subln_gate(o, sub_ref[...], bg_ref[:, h * DV_B:(h + 1) * DV_B], lam_init))
        o_ref[...] = jnp.concatenate(outs, axis=-1)


def _attn_sample(page_table, q, cache_k, cache_v, kn, vn, bg, sub, lq1, lk1, lq2, lk2, lam_init,
                 layer, npg):
    nb, n_pages = page_table.shape
    n_steps = n_pages // npg
    seq = lambda b, p, pt: (b, 0)

    def page(i):
        return lambda b, p, pt: (layer, pt[b, jnp.minimum(p, n_steps - 1) * npg + i], 0, 0)

    full = lambda shp: pl.BlockSpec(shp, lambda b, p, pt: (0,) * len(shp))
    pages = [pl.BlockSpec((None, None, PAGE * H_B, DV_B), page(i)) for i in range(npg)]
    tok = pl.BlockSpec((SROWS, W_B), seq)
    new = pl.BlockSpec((SROWS * H_B, DV_B), seq)
    grid_spec = pltpu.PrefetchScalarGridSpec(
        num_scalar_prefetch=1,
        grid=(nb, n_steps + 1),
        in_specs=[tok] + pages + pages + [new, new, tok, full((1, DV_B)), full((1, DK_B)),
                                          full((1, DK_B)), full((1, DK_B)), full((1, DK_B))],
        out_specs=tok,
        scratch_shapes=[pltpu.VMEM((H_B, N_QROWS, DV_B), F32), pltpu.VMEM((H_B, N_QROWS, 1), F32),
                        pltpu.VMEM((H_B, N_QROWS, 1), F32), pltpu.VMEM((H_B, N_QROWS, DV_B), F32)],
    )
    return pl.pallas_call(
        functools.partial(_attn_sample_kernel, lam_init=lam_init, n_steps=n_steps, npg=npg),
        grid_spec=grid_spec,
        out_shape=jax.ShapeDtypeStruct((nb * SROWS, W_B), F32),
        compiler_params=_cparams(("parallel", "arbitrary")),
        name="attn_sample",
    )(page_table, q, *([cache_k] * npg), *([cache_v] * npg), kn, vn, bg, sub, lq1, lk1, lq2, lk2)


def _dot_hi(a, b):
    return jnp.dot(a, b, preferred_element_type=F32, precision=HI)


def _split(x):
    hi = x.astype(BF16)
    return hi, (x - hi.astype(F32)).astype(BF16)


def _dot3(a, b):
    ah, al = a
    bh, bl = b
    return _dot(ah, bh) + (_dot(ah, bl) + _dot(al, bh))


def _tri_inverse_many(a_list, c):
    row = lax.broadcasted_iota(jnp.int32, (c, c), 0)
    col = lax.broadcasted_iota(jnp.int32, (c, c), 1)
    eye = (row == col).astype(F32)
    nb = min(c, 16)
    ad = [jnp.where(row // nb == col // nb, a, 0.0) for a in a_list]
    t = [eye - x for x in ad]
    pw = ad
    size = 2
    while size < nb:
        ps = _each(_split, pw)
        pw = _each(_dot3, ps, ps)
        tp = _each(_dot3, _each(_split, t), _each(_split, pw))
        t = _each(lambda x, y: x + y, t, tp)
        size *= 2
    size = nb * 2
    while size <= c:
        off = jnp.logical_and(row // size == col // size, row // (size // 2) != col // (size // 2))
        ts = _each(_split, t)
        inner = _each(_dot3, [_split(jnp.where(off, a, 0.0)) for a in a_list], ts)
        outer = _each(_dot3, ts, _each(_split, inner))
        t = _each(lambda x, y: x - y, t, outer)
        size *= 2
    return t


def _delta_prep_many(items, c):
    row = lax.broadcasted_iota(jnp.int32, (c, c), 0)
    col = lax.broadcasted_iota(jnp.int32, (c, c), 1)
    decay = [jnp.exp(jnp.where(row >= col, it[3] - it[4], -jnp.inf)) for it in items]
    kb = [it[1].astype(BF16) for it in items]
    kk = _each(_dot_nt, kb, kb)
    a = [jnp.where(row > col, it[5] * x * d, 0.0) for it, x, d in zip(items, kk, decay)]
    t = _tri_inverse_many(a, c)
    eg = [jnp.exp(it[3]) for it in items]
    rhs = [_split(jnp.concatenate([it[2] * it[5], it[1] * (it[5] * e)], axis=-1))
           for it, e in zip(items, eg)]
    uw = _each(_dot3, _each(_split, t), rhs)
    qk = _each(_dot_nt, [it[0].astype(BF16) for it in items], kb)
    out = []
    for it, x, e, s, d in zip(items, uw, eg, qk, decay):
        q, k, _, gcol, _, _ = it
        glast = gcol[c - 1:c, :]
        wq = jnp.concatenate([x[:, DV_C:], q * e], axis=0).astype(BF16)
        kt = (k * jnp.exp(glast - gcol)).astype(BF16)
        out.append((x[:, :DV_C], wq, (s * d).astype(BF16), kt, jnp.exp(glast)))
    return out


def _delta_scan(u, wq, qk, kt, gl, s, c):
    r = _dot(wq, s.astype(BF16))
    vnb = (u - r[:c]).astype(BF16)
    return r[c:] + _dot(qk, vnb), s * gl + _dot_tn(kt, vnb)


def _l2norm(x):
    return x * lax.rsqrt(jnp.sum(x * x, axis=-1, keepdims=True) + EPS)


def _gdn_block(x, cm, alog, dtb, onw, states, c, nseg, chained, valid):
    rows = nseg * c
    beta = _sigmoid(cm[:, W_C:W_C + LANES])
    g = -jnp.exp(alog) * _softplus(cm[:, W_C + LANES:W_C + 2 * LANES] + dtb)
    if valid is not None:
        beta = jnp.where(valid, beta, 0.0)
        g = jnp.where(valid, g, 0.0)
        x = jnp.where(valid, x, 0.0)
    row = lax.broadcasted_iota(jnp.int32, (rows, rows), 0)
    col = lax.broadcasted_iota(jnp.int32, (rows, rows), 1)
    same = row // c == col // c
    gcs = _dot_hi(jnp.logical_and(same, row >= col).astype(F32), g)
    gcs_t = lax.dot_general(g, jnp.logical_and(same, row <= col).astype(F32),
                            (((0,), (0,)), ((), ())), preferred_element_type=F32,
                            precision=HI)
    qn = [_l2norm(x[:, h * DK_C:(h + 1) * DK_C]) * (DK_C ** -0.5) for h in range(H_C)]
    kn = [_l2norm(x[:, W_C + h * DK_C:W_C + (h + 1) * DK_C]) for h in range(H_C)]
    keys = [(i, h) for i in range(nseg) for h in range(H_C)]
    seg = lambda a, i: a[i * c:(i + 1) * c]
    items = [(seg(qn[h], i), seg(kn[h], i),
              seg(x, i)[:, 2 * W_C + h * DV_C:2 * W_C + (h + 1) * DV_C],
              seg(gcs, i)[:, h:h + 1], gcs_t[h:h + 1, i * c:(i + 1) * c],
              seg(beta, i)[:, h:h + 1]) for i, h in keys]
    preps = dict(zip(keys, _delta_prep_many(items, c)))
    s_cur = list(states)
    ys = []
    for i in range(nseg):
        outs = []
        for h in range(H_C):
            si = h if chained else i * H_C + h
            o, s_cur[si] = _delta_scan(*preps[i, h], s_cur[si], c)
            o = o * lax.rsqrt(jnp.mean(o * o, axis=-1, keepdims=True) + EPS) * onw
            outs.append(o * _silu(seg(cm, i)[:, h * DV_C:(h + 1) * DV_C]))
        ys.append(jnp.concatenate(outs, axis=-1))
    return jnp.concatenate(ys, axis=0), s_cur


CONVC_PRE = SUBLANES
CONVC_OFF = CONVC_PRE - (SHORT_CONV - 1)
GDN_CHUNKS = 4


def _gdn_prompt_kernel(x_ref, cm_ref, cw_ref, alog_ref, dtb_ref, onw_ref, y_ref, s_ref,
                       ext_ref, st_ref, *, nch):
    i = pl.program_id(1)
    rows = nch * CHUNK

    @pl.when(i == 0)
    def _():
        ext_ref[0:CONVC_PRE, :] = jnp.zeros((CONVC_PRE, QKV_C), F32)
        st_ref[...] = jnp.zeros(st_ref.shape, F32)

    ext_ref[CONVC_PRE:CONVC_PRE + rows, :] = x_ref[...]
    acc = jnp.zeros((rows, QKV_C), F32)
    for j in range(SHORT_CONV):
        acc = acc + cw_ref[j:j + 1, :] * ext_ref[CONVC_OFF + j:CONVC_OFF + j + rows, :]
    ext_ref[0:CONVC_PRE, :] = ext_ref[rows:rows + CONVC_PRE, :]
    y, s_new = _gdn_block(_silu(acc), cm_ref[...], alog_ref[...], dtb_ref[...], onw_ref[...],
                          [st_ref[h] for h in range(H_C)], CHUNK, nch, True, None)
    y_ref[...] = y.astype(BF16)
    for h in range(H_C):
        st_ref[h] = s_new[h]
    s_ref[...] = st_ref[...]


def _gdn_prompt(x, cm, cw, alog, dtb, onw, bsz, seq, nch):
    rows = nch * CHUNK
    nc = seq // rows
    full = lambda shp: pl.BlockSpec(shp, lambda b, i: (0,) * len(shp))
    tok = lambda wd: pl.BlockSpec((rows, wd), lambda b, i: (b * nc + i, 0))
    return pl.pallas_call(
        functools.partial(_gdn_prompt_kernel, nch=nch),
        grid=(bsz, nc),
        in_specs=[tok(QKV_C), tok(W_CM), full((SHORT_CONV, QKV_C)), full((1, LANES)),
                  full((1, LANES)), full((1, DV_C))],
        out_specs=[tok(W_C), pl.BlockSpec((None, H_C, DK_C, DV_C), lambda b, i: (b, 0, 0, 0))],
        out_shape=[jax.ShapeDtypeStruct((bsz * seq, W_C), BF16),
                   jax.ShapeDtypeStruct((bsz, H_C, DK_C, DV_C), F32)],
        scratch_shapes=[pltpu.VMEM((CONVC_PRE + rows, QKV_C), F32),
                        pltpu.VMEM((H_C, DK_C, DV_C), F32)],
        compiler_params=_cparams(("parallel", "arbitrary")),
        name="gdn_prompt",
    )(x, cm, cw, alog, dtb, onw)


def _gdn_sample_kernel(x_ref, st_ref, cm_ref, s0_ref, cw_ref, alog_ref, dtb_ref, onw_ref,
                       y_ref, s_ref, *, n_real):
    accs = []
    for g in range(SEQ_GROUP):
        ext = jnp.concatenate([st_ref[g], x_ref[g * SROWS:(g + 1) * SROWS, :]], axis=0)
        acc = jnp.zeros((SROWS, QKV_C), F32)
        for j in range(SHORT_CONV):
            acc = acc + cw_ref[j:j + 1, :] * ext[CONVC_OFF + j:CONVC_OFF + j + SROWS, :]
        accs.append(acc)
    rows = SEQ_GROUP * SROWS
    valid = lax.broadcasted_iota(jnp.int32, (rows, 1), 0) % SROWS < n_real
    states = [s0_ref[g, h] for g in range(SEQ_GROUP) for h in range(H_C)]
    y, s_new = _gdn_block(_silu(jnp.concatenate(accs, axis=0)), cm_ref[...], alog_ref[...],
                          dtb_ref[...], onw_ref[...], states, SROWS, SEQ_GROUP, False, valid)
    y_ref[...] = y
    for g in range(SEQ_GROUP):
        for h in range(H_C):
            s_ref[g, h] = s_new[g * H_C + h]


def _gdn_sample(x, st, cm, s0, cw, alog, dtb, onw, n_real):
    nb = s0.shape[0]
    rows = SEQ_GROUP * SROWS
    full = lambda shp: pl.BlockSpec(shp, lambda g: (0,) * len(shp))
    tok = lambda wd: pl.BlockSpec((rows, wd), lambda g: (g, 0))
    state = pl.BlockSpec((SEQ_GROUP, H_C, DK_C, DV_C), lambda g: (g, 0, 0, 0))
    return pl.pallas_call(
        functools.partial(_gdn_sample_kernel, n_real=n_real),
        grid=(nb // SEQ_GROUP,),
        in_specs=[tok(QKV_C), pl.BlockSpec((SEQ_GROUP, CONVC_PRE, QKV_C), lambda g: (g, 0, 0)),
                  tok(W_CM), state, full((SHORT_CONV, QKV_C)), full((1, LANES)), full((1, LANES)),
                  full((1, DV_C))],
        out_specs=[tok(W_C), state],
        out_shape=[jax.ShapeDtypeStruct((nb * SROWS, W_C), F32),
                   jax.ShapeDtypeStruct((nb, H_C, DK_C, DV_C), F32)],
        compiler_params=_cparams(("parallel",)),
        name="gdn_sample",
    )(x, st, cm, s0, cw, alog, dtb, onw)


def _prepare_w_in(w_in):
    depth, d, _ = w_in.shape
    beta = w_in[:, :, C_CG:C_CG + H_C]
    alpha = w_in[:, :, C_CG + H_C:C_CG + 2 * H_C]
    gate = w_in[:, :, C_CG + 2 * H_C:]
    zpad = jnp.zeros((depth, d, LANES - H_C), w_in.dtype)
    wtail = jnp.concatenate([gate, beta, zpad, alpha, zpad], axis=-1).astype(BF16)
    wt = jnp.concatenate([w_in[:, :, C_Q:C_K], w_in[:, :, C_V:C_BG]], axis=-1)
    return w_in.astype(BF16), wtail, jnp.swapaxes(wt, 1, 2).astype(BF16)


def _pad_lanes(x):
    return jnp.pad(x, ((0, 0), (0, LANES - x.shape[-1])))


def kernel(x_prompt, x_sample, cache_k, cache_v, state_conv_a, state_conv_c, state_delta, page_table,
           norm_w, w_in, conv_a_w, conv_a_b, ln_a_w, ln_a_b, pw_a_w, pw_a_b,
           lambda_q1, lambda_k1, lambda_q2, lambda_k2, subln_w, conv_c_w, a_log, dt_bias,
           onorm_c_w, w_out, final_norm_w):
    bsz, seq, d = x_prompt.shape
    nb, ns, _ = x_sample.shape
    depth = w_in.shape[0]
    n_pool = cache_k.shape[1]
    n_pages = page_table.shape[1]
    t_attn = ATTN_TILE if seq % ATTN_TILE == 0 else seq
    tl_conv = 512 if seq % 512 == 0 else seq
    tm_p = 256
    nch = GDN_CHUNKS if seq % (GDN_CHUNKS * CHUNK) == 0 else 1
    npg = PAGES_PER_STEP if n_pages % PAGES_PER_STEP == 0 else 1
    assert SHORT_CONV - 1 <= ns <= SROWS, "sample tokens must fit one 8-row tile"
    assert nb % SEQ_GROUP == 0 and seq % CHUNK == 0 and seq % t_attn == 0 and (bsz * seq) % tm_p == 0
    assert tl_conv % CONV_RC == 0 and tl_conv >= CONV_PRE

    wp, wtail, wt = _prepare_w_in(w_in)
    wo = w_out.astype(BF16)
    pw = pw_a_w.astype(BF16)
    ck = cache_k.reshape(depth, n_pool, PAGE * H_B, DV_B)
    cv = cache_v.reshape(depth, n_pool, PAGE * H_B, DV_B)
    st_a = jnp.pad(state_conv_a, ((0, 0), (0, 0), (CONV_PRE - (CONV_A - 1), 0), (0, 0)))
    st_c = jnp.pad(state_conv_c, ((0, 0), (0, 0), (CONVC_PRE - (SHORT_CONV - 1), 0), (0, 0)))
    alog = _pad_lanes(a_log)
    dtb = _pad_lanes(dt_bias)
    fw = final_norm_w.reshape(1, d)

    hp = x_prompt.reshape(bsz * seq, d)
    hs = jnp.pad(x_sample, ((0, 0), (0, SROWS - ns), (0, 0))).reshape(nb * SROWS, d)
    outs_p = [[] for _ in range(5)]
    outs_s = [[] for _ in range(5)]
    for l in range(depth):
        lam_init = 0.8 - 0.6 * math.exp(-0.3 * l)
        final = l == depth - 1
        row = lambda a: a[l].reshape(1, -1)
        lams = (row(lambda_q1), row(lambda_k1), row(lambda_q2), row(lambda_k2))

        a3, qt, k32, k16, v32, vt, bg, qkvc, cm = _proj(hp, row(norm_w), wp, wtail, wt, l, tm_p, True)
        ya, tail = _conva_prompt(a3, conv_a_w[l], row(conv_a_b), row(ln_a_w), row(ln_a_b), pw[l],
                                 row(pw_a_b), bsz, seq, tl_conv)
        yb = _attn_prompt(qt, k16, vt, bg, row(subln_w), *lams, lam_init, bsz, seq, t_attn)
        yc, s_new = _gdn_prompt(qkvc, cm, conv_c_w[l], alog[l:l + 1], dtb[l:l + 1], row(onorm_c_w),
                                bsz, seq, nch)
        hp = _outproj(ya, yb, yc, hp, wo[l], fw, tm_p, final)
        outs_p[0].append(k32.reshape(bsz, seq, H_B, 2 * DK_B))
        outs_p[1].append(v32.reshape(bsz, seq, H_B, DV_B))
        outs_p[2].append(tail[:, CONV_PRE - (CONV_A - 1):])
        outs_p[3].append(qkvc.reshape(bsz, seq, QKV_C)[:, seq - (SHORT_CONV - 1):])
        outs_p[4].append(s_new)

        a3, q, k32, v32, bg, qkvc, cm = _proj(hs, row(norm_w), wp, wtail, wt, l, nb * SROWS, False)
        ya, u = _conva_sample(a3.reshape(nb, SROWS, 3 * W_A), st_a[l], conv_a_w[l], row(conv_a_b),
                              row(ln_a_w), row(ln_a_b), pw[l], row(pw_a_b))
        yb = _attn_sample(page_table, q, ck, cv, k32, v32, bg, row(subln_w), *lams, lam_init, l, npg)
        yc, s_new = _gdn_sample(qkvc, st_c[l], cm, state_delta[l], conv_c_w[l], alog[l:l + 1],
                                dtb[l:l + 1], row(onorm_c_w), ns)
        hs = _outproj(ya.reshape(nb * SROWS, W_A), yb, yc, hs, wo[l], fw, nb * SROWS, final)
        outs_s[0].append(k32.reshape(nb, SROWS, H_B, 2 * DK_B)[:, :ns])
        outs_s[1].append(v32.reshape(nb, SROWS, H_B, DV_B)[:, :ns])
        outs_s[2].append(jnp.concatenate([state_conv_a[l], u[:, :ns]], axis=1)[:, -(CONV_A - 1):])
        outs_s[3].append(jnp.concatenate([state_conv_c[l], qkvc.reshape(nb, SROWS, QKV_C)[:, :ns]],
                                         axis=1)[:, -(SHORT_CONV - 1):])
        outs_s[4].append(s_new)

    y_prompt = hp.reshape(bsz, seq, d)
    y_sample = hs.reshape(nb, SROWS, d)[:, :ns]
    stack = lambda xs: jnp.stack(xs)
    return (y_prompt, y_sample,
            stack(outs_p[0]), stack(outs_p[1]), stack(outs_p[2]), stack(outs_p[3]),
            stack(outs_p[4]).astype(state_delta.dtype),
            stack(outs_s[0]), stack(outs_s[1]), stack(outs_s[2]), stack(outs_s[3]),
            stack(outs_s[4]).astype(state_delta.dtype))
```

```python
import functools
import math

import jax
import jax.numpy as jnp
from jax import lax
from jax.experimental import pallas as pl
from jax.experimental.pallas import tpu as pltpu

F32 = jnp.float32
BF16 = jnp.bfloat16

EPS = 1e-6
LOG2E = math.log2(math.e)
W_A = 256
CONV_A = 31
H_B = 4
DK_B = 64
DV_B = 128
W_B = H_B * DV_B
H_C = 4
DK_C = 64
DV_C = 64
W_C = H_C * DV_C
QKV_C = H_C * (2 * DK_C + DV_C)
SHORT_CONV = 4
CHUNK = 64
PAGE = 128
D_MIX = W_A + W_B + W_C

LANES = 128
SUBLANES = 8
SROWS = SUBLANES
VMEM_LIMIT = 48 * 1024 * 1024

C_A3 = 0
C_Q = C_A3 + 3 * W_A
C_K = C_Q + W_B
C_V = C_K + W_B
C_BG = C_V + W_B
C_QKVC = C_BG + W_B
C_CG = C_QKVC + QKV_C
W_CM = W_C + 2 * LANES


def _cparams(sem):
    return pltpu.CompilerParams(dimension_semantics=sem, vmem_limit_bytes=VMEM_LIMIT)


def _sigmoid(x):
    return 1.0 / (1.0 + jnp.exp(-x))


def _silu(x):
    return x * _sigmoid(x)


def _softplus(x):
    return jnp.maximum(x, 0.0) + jnp.log1p(jnp.exp(-jnp.abs(x)))


def _dot(a, b):
    return jnp.dot(a, b, preferred_element_type=F32)


def _dot_nt(a, b):
    return lax.dot_general(a, b, (((1,), (1,)), ((), ())), preferred_element_type=F32)


def _dot_tn(a, b):
    return lax.dot_general(a, b, (((0,), (0,)), ((), ())), preferred_element_type=F32)


def _each(f, *lists):
    return [f(*xs) for xs in zip(*lists)]


def _proj_kernel(x_ref, nw_ref, w_ref, wtail_ref, wt_ref, *out_refs, prompt):
    x = x_ref[...]
    ms = jnp.mean(x * x, axis=-1, keepdims=True)
    hn = (x * lax.rsqrt(ms + EPS) * nw_ref[...]).astype(BF16)

    def mm(lo, hi):
        return _dot(hn, w_ref[:, lo:hi])

    if prompt:
        a3_ref, qt_ref, k32_ref, k16_ref, v32_ref, vt_ref, bg_ref, qkvc_ref, cm_ref = out_refs
        qt_ref[...] = (_dot_nt(wt_ref[0:W_B, :], hn) * (LOG2E * DK_B ** -0.5)).astype(BF16)
        vt = _dot_nt(wt_ref[W_B:2 * W_B, :], hn)
        vt_ref[...] = vt.astype(BF16)
        v = vt.T
        k = mm(C_K, C_V)
        k16_ref[...] = k.astype(BF16)
    else:
        a3_ref, q_ref, k32_ref, v32_ref, bg_ref, qkvc_ref, cm_ref = out_refs
        q_ref[...] = mm(C_Q, C_K) * (DK_B ** -0.5)
        k = mm(C_K, C_V)
        v = mm(C_V, C_BG)
    a3_ref[...] = mm(C_A3, C_Q)
    tm = x.shape[0]
    for h in range(H_B):
        k32_ref[pl.ds(h, tm, stride=H_B), :] = k[:, h * DV_B:(h + 1) * DV_B]
        v32_ref[pl.ds(h, tm, stride=H_B), :] = v[:, h * DV_B:(h + 1) * DV_B]
    bg_ref[...] = mm(C_BG, C_QKVC)
    qkvc_ref[...] = mm(C_QKVC, C_CG)
    cm_ref[...] = _dot(hn, wtail_ref[...])


def _proj(x, nw, w, wtail, wt, layer, tm, prompt):
    m, d = x.shape
    n_in = w.shape[-1]
    per_layer = lambda r, c: pl.BlockSpec((None, r, c), lambda i: (layer, 0, 0))
    tok = lambda wd, dt: (pl.BlockSpec((tm, wd), lambda i: (i, 0)), jax.ShapeDtypeStruct((m, wd), dt))
    tr = lambda: (pl.BlockSpec((W_B, tm), lambda i: (0, i)), jax.ShapeDtypeStruct((W_B, m), BF16))
    hd = lambda: (pl.BlockSpec((tm * H_B, DV_B), lambda i: (i, 0)),
                  jax.ShapeDtypeStruct((m * H_B, DV_B), F32))
    if prompt:
        outs = [tok(3 * W_A, F32), tr(), hd(), tok(W_B, BF16), hd(), tr(),
                tok(W_B, F32), tok(QKV_C, F32), tok(W_CM, F32)]
    else:
        outs = [tok(3 * W_A, F32), tok(W_B, F32), hd(), hd(),
                tok(W_B, F32), tok(QKV_C, F32), tok(W_CM, F32)]
    return pl.pallas_call(
        functools.partial(_proj_kernel, prompt=prompt),
        grid=(m // tm,),
        in_specs=[pl.BlockSpec((tm, d), lambda i: (i, 0)),
                  pl.BlockSpec((1, d), lambda i: (0, 0)),
                  per_layer(d, n_in), per_layer(d, W_CM), per_layer(2 * W_B, d)],
        out_specs=[o[0] for o in outs],
        out_shape=[o[1] for o in outs],
        compiler_params=_cparams(("parallel",)),
        name="proj",
    )(x, nw, w, wtail, wt)


def _outproj_kernel(ya_ref, yb_ref, yc_ref, h_ref, w_ref, fw_ref, o_ref, *, final):
    ycat = jnp.concatenate([ya_ref[...].astype(BF16), yb_ref[...].astype(BF16),
                            yc_ref[...].astype(BF16)], axis=-1)
    hnew = h_ref[...] + _dot(ycat, w_ref[...])
    if final:
        ms = jnp.mean(hnew * hnew, axis=-1, keepdims=True)
        hnew = hnew * lax.rsqrt(ms + EPS) * fw_ref[...]
    o_ref[...] = hnew


def _outproj(ya, yb, yc, h, w, fw, tm, final):
    m, d = h.shape
    row = lambda wd: pl.BlockSpec((tm, wd), lambda i: (i, 0))
    return pl.pallas_call(
        functools.partial(_outproj_kernel, final=final),
        grid=(m // tm,),
        in_specs=[row(W_A), row(W_B), row(W_C), row(d),
                  pl.BlockSpec((D_MIX, d), lambda i: (0, 0)),
                  pl.BlockSpec((1, d), lambda i: (0, 0))],
        out_specs=row(d),
        out_shape=jax.ShapeDtypeStruct((m, d), F32),
        compiler_params=_cparams(("parallel",)),
        name="outproj",
    )(ya, yb, yc, h, w, fw)


def _conva_tail(conv, a_gate, cb, lnw, lnb, pw, pb):
    y = conv + cb
    mu = jnp.mean(y, axis=-1, keepdims=True)
    yc = y - mu
    y = yc * lax.rsqrt(jnp.mean(yc * yc, axis=-1, keepdims=True) + EPS) * lnw + lnb
    y = _silu(y)
    y = _dot(y.astype(BF16), pw) + pb
    return y * _silu(a_gate)


CONV_PRE = 32
CONV_OFF = CONV_PRE - (CONV_A - 1)
CONV_RC = 64


def _conva_prompt_kernel(a3_ref, cw_ref, cb_ref, lnw_ref, lnb_ref, pw_ref, pb_ref,
                         ya_ref, tail_ref, ext_ref, *, tl):
    i = pl.program_id(1)

    @pl.when(i == 0)
    def _():
        ext_ref[0, 0:CONV_PRE, :] = jnp.zeros((CONV_PRE, W_A), F32)

    u = a3_ref[:, 0:W_A] * _sigmoid(a3_ref[:, W_A:2 * W_A])
    ext_ref[0, CONV_PRE:CONV_PRE + tl, :] = u
    n_shift = CONV_PRE + tl - SUBLANES
    for s in range(1, SUBLANES):
        ext_ref[s, 0:n_shift, :] = ext_ref[0, s:s + n_shift, :]
    for r in range(tl // CONV_RC):
        acc = jnp.zeros((CONV_RC, W_A), F32)
        for j in range(CONV_A):
            off = r * CONV_RC + CONV_OFF + j
            s = off % SUBLANES
            acc = acc + cw_ref[j:j + 1, :] * ext_ref[s, off - s:off - s + CONV_RC, :]
        gate = a3_ref[r * CONV_RC:(r + 1) * CONV_RC, 2 * W_A:3 * W_A]
        y = _conva_tail(acc, gate, cb_ref[...], lnw_ref[...], lnb_ref[...], pw_ref[...], pb_ref[...])
        ya_ref[r * CONV_RC:(r + 1) * CONV_RC, :] = y.astype(BF16)
    last = ext_ref[0, tl:tl + CONV_PRE, :]
    tail_ref[...] = last
    ext_ref[0, 0:CONV_PRE, :] = last


def _conva_prompt(a3, cw, cb, lnw, lnb, pw, pb, bsz, seq, tl):
    nl = seq // tl
    full = lambda shp: pl.BlockSpec(shp, lambda b, i: (0,) * len(shp))
    return pl.pallas_call(
        functools.partial(_conva_prompt_kernel, tl=tl),
        grid=(bsz, nl),
        in_specs=[pl.BlockSpec((tl, 3 * W_A), lambda b, i: (b * nl + i, 0)),
                  full((CONV_A, W_A)), full((1, W_A)), full((1, W_A)), full((1, W_A)),
                  full((W_A, W_A)), full((1, W_A))],
        out_specs=[pl.BlockSpec((tl, W_A), lambda b, i: (b * nl + i, 0)),
                   pl.BlockSpec((None, CONV_PRE, W_A), lambda b, i: (b, 0, 0))],
        out_shape=[jax.ShapeDtypeStruct((bsz * seq, W_A), BF16),
                   jax.ShapeDtypeStruct((bsz, CONV_PRE, W_A), F32)],
        scratch_shapes=[pltpu.VMEM((SUBLANES, CONV_PRE + tl, W_A), F32)],
        compiler_params=_cparams(("parallel", "arbitrary")),
        name="conva_prompt",
    )(a3, cw, cb, lnw, lnb, pw, pb)


SEQ_GROUP = 8


def _conva_sample_kernel(a3_ref, st_ref, cw_ref, cb_ref, lnw_ref, lnb_ref, pw_ref, pb_ref,
                         ya_ref, u_ref, ext_ref):
    a3 = a3_ref[...]
    u = a3[:, :, 0:W_A] * _sigmoid(a3[:, :, W_A:2 * W_A])
    u_ref[...] = u
    ext_ref[:, 0:CONV_PRE, :] = st_ref[...]
    ext_ref[:, CONV_PRE:CONV_PRE + SROWS, :] = u
    acc = jnp.zeros((SEQ_GROUP, SROWS, W_A), F32)
    for j in range(CONV_A):
        acc = acc + cw_ref[j:j + 1, :] * ext_ref[:, CONV_OFF + j:CONV_OFF + j + SROWS, :]
    rows = SEQ_GROUP * SROWS
    y = _conva_tail(acc.reshape(rows, W_A), a3[:, :, 2 * W_A:3 * W_A].reshape(rows, W_A),
                    cb_ref[...], lnw_ref[...], lnb_ref[...], pw_ref[...], pb_ref[...])
    ya_ref[...] = y.reshape(SEQ_GROUP, SROWS, W_A)


def _conva_sample(a3, st, cw, cb, lnw, lnb, pw, pb):
    nb = a3.shape[0]
    full = lambda shp: pl.BlockSpec(shp, lambda g: (0,) * len(shp))
    seqs = lambda r, wd: pl.BlockSpec((SEQ_GROUP, r, wd), lambda g: (g, 0, 0))
    return pl.pallas_call(
        _conva_sample_kernel,
        grid=(nb // SEQ_GROUP,),
        in_specs=[seqs(SROWS, 3 * W_A), seqs(CONV_PRE, W_A),
                  full((CONV_A, W_A)), full((1, W_A)), full((1, W_A)), full((1, W_A)),
                  full((W_A, W_A)), full((1, W_A))],
        out_specs=[seqs(SROWS, W_A), seqs(SROWS, W_A)],
        out_shape=[jax.ShapeDtypeStruct((nb, SROWS, W_A), F32),
                   jax.ShapeDtypeStruct((nb, SROWS, W_A), F32)],
        scratch_shapes=[pltpu.VMEM((SEQ_GROUP, CONV_PRE + SROWS, W_A), F32)],
        compiler_params=_cparams(("parallel",)),
        name="conva_sample",
    )(a3, st, cw, cb, lnw, lnb, pw, pb)


def _lambda_full(lq1, lk1, lq2, lk2, lam_init):
    s1 = jnp.sum(lq1 * lk1, axis=-1, keepdims=True)
    s2 = jnp.sum(lq2 * lk2, axis=-1, keepdims=True)
    return jnp.exp(s1) - jnp.exp(s2) + lam_init


def _subln_gate(o, sub, bg, lam_init):
    y = o * lax.rsqrt(jnp.mean(o * o, axis=-1, keepdims=True) + EPS) * sub
    return (y * (1.0 - lam_init)) * _silu(bg)


ATTN_TILE = 1024
ATTN_STRIP = 256


def _attn_prompt_kernel(qt_ref, k_ref, vt_ref, bg_ref, sub_ref, lq1_ref, lk1_ref, lq2_ref, lk2_ref,
                        o_ref, m1_ref, l1_ref, a1_ref, m2_ref, l2_ref, a2_ref, *, lam_init, t):
    qi = pl.program_id(2)
    qt = qt_ref[...]
    feat = lax.broadcasted_iota(jnp.int32, qt.shape, 0)
    zero = jnp.zeros_like(qt)
    qs = (jnp.where(feat < DK_B, qt, zero), jnp.where(feat >= DK_B, qt, zero))
    stats = ((m1_ref, l1_ref, a1_ref), (m2_ref, l2_ref, a2_ref))
    for m_ref, l_ref, a_ref in stats:
        m_ref[...] = jnp.full(m_ref.shape, -jnp.inf, F32)
        l_ref[...] = jnp.zeros(l_ref.shape, F32)
        a_ref[...] = jnp.zeros(a_ref.shape, F32)

    sw = min(ATTN_STRIP, t)
    units = [(mp, c) for c in range(t // sw) for mp in (0, 1)]

    def tile(j, causal):
        start = pl.multiple_of(j * t, t)
        k = k_ref[pl.ds(start, t), :]
        vt = vt_ref[:, pl.ds(start, t)]
        nkeys = lambda c: (c + 1) * sw if causal else t

        def scores(u):
            mp, c = u
            return _dot(k[:nkeys(c)], qs[mp][:, c * sw:(c + 1) * sw])

        def softmax(u, st):
            mp, c = u
            m_ref, l_ref, _ = stats[mp]
            cols = slice(c * sw, (c + 1) * sw)
            if causal:
                kr = lax.broadcasted_iota(jnp.int32, st.shape, 0)
                qc = lax.broadcasted_iota(jnp.int32, st.shape, 1) + c * sw
                st = jnp.where(kr <= qc, st, -jnp.inf)
            m_prev = m_ref[:, cols]
            m_new = jnp.maximum(m_prev, jnp.max(st, axis=0, keepdims=True))
            alpha = jnp.exp2(m_prev - m_new)
            pt = jnp.exp2(st - m_new)
            l_ref[:, cols] = alpha * l_ref[:, cols] + jnp.sum(pt, axis=0, keepdims=True)
            m_ref[:, cols] = m_new
            return pt.astype(BF16), alpha

        def weighted_values(u, pt, alpha):
            mp, c = u
            a_ref = stats[mp][2]
            cols = slice(c * sw, (c + 1) * sw)
            a_ref[:, cols] = alpha * a_ref[:, cols] + _dot(vt[:, :nkeys(c)], pt)

        n = len(units)
        sts, pts = {}, {}
        for i in range(n + 2):
            if i < n:
                sts[i] = scores(units[i])
            if 0 <= i - 1 < n:
                pts[i - 1] = softmax(units[i - 1], sts.pop(i - 1))
            if 0 <= i - 2 < n:
                weighted_values(units[i - 2], *pts.pop(i - 2))

    def body(j, carry):
        tile(j, False)
        return carry

    lax.fori_loop(0, qi, body, 0)
    tile(qi, True)
    lam = _lambda_full(lq1_ref[...], lk1_ref[...], lq2_ref[...], lk2_ref[...], lam_init)
    ot = a1_ref[...] / l1_ref[...] - lam * (a2_ref[...] / l2_ref[...])
    o_ref[...] = _subln_gate(ot.T, sub_ref[...], bg_ref[...], lam_init).astype(BF16)


def _attn_prompt(qt, k, vt, bg, sub, lq1, lk1, lq2, lk2, lam_init, bsz, seq, t):
    n = seq // t
    full = lambda shp: pl.BlockSpec(shp, lambda b, h, qi: (0,) * len(shp))
    return pl.pallas_call(
        functools.partial(_attn_prompt_kernel, lam_init=lam_init, t=t),
        grid=(bsz, H_B, n),
        in_specs=[pl.BlockSpec((DV_B, t), lambda b, h, qi: (h, b * n + qi)),
                  pl.BlockSpec((seq, DV_B), lambda b, h, qi: (b, h)),
                  pl.BlockSpec((DV_B, seq), lambda b, h, qi: (h, b)),
                  pl.BlockSpec((t, DV_B), lambda b, h, qi: (b * n + qi, h)),
                  full((1, DV_B)), full((1, DK_B)), full((1, DK_B)), full((1, DK_B)),
                  full((1, DK_B))],
        out_specs=pl.BlockSpec((t, DV_B), lambda b, h, qi: (b * n + qi, h)),
        out_shape=jax.ShapeDtypeStruct((bsz * seq, W_B), BF16),
        scratch_shapes=[pltpu.VMEM((1, t), F32), pltpu.VMEM((1, t), F32), pltpu.VMEM((DV_B, t), F32),
                        pltpu.VMEM((1, t), F32), pltpu.VMEM((1, t), F32), pltpu.VMEM((DV_B, t), F32)],
        compiler_params=_cparams(("parallel", "parallel", "arbitrary")),
        name="attn_prompt",
    )(qt, k, vt, bg, sub, lq1, lk1, lq2, lk2)


PAGES_PER_STEP = 16
N_QROWS = 2 * SROWS


def _attn_sample_kernel(pt_ref, q_ref, *refs, lam_init, n_steps, npg):
    kp_refs = refs[:npg]
    vp_refs = refs[npg:2 * npg]
    (kn_ref, vn_ref, bg_ref, sub_ref, lq1_ref, lk1_ref, lq2_ref, lk2_ref, o_ref,
     qb_ref, m_ref, l_ref, acc_ref) = refs[2 * npg:]
    p = pl.program_id(1)
    heads = range(H_B)

    @pl.when(p == 0)
    def _():
        q = q_ref[...]
        lane = lax.broadcasted_iota(jnp.int32, (SROWS, DV_B), 1)
        for h in heads:
            qh = q[:, h * DV_B:(h + 1) * DV_B]
            qb_ref[h, 0:SROWS, :] = jnp.where(lane < DK_B, qh, 0.0)
            qb_ref[h, SROWS:N_QROWS, :] = jnp.where(lane >= DK_B, qh, 0.0)
        m_ref[...] = jnp.full(m_ref.shape, -jnp.inf, F32)
        l_ref[...] = jnp.zeros(l_ref.shape, F32)
        acc_ref[...] = jnp.zeros(acc_ref.shape, F32)

    def update_all(ss, vs):
        m_prev = [m_ref[h] for h in heads]
        m_new = [jnp.maximum(mp, jnp.max(s, axis=-1, keepdims=True)) for mp, s in zip(m_prev, ss)]
        alpha = [jnp.exp(mp - mn) for mp, mn in zip(m_prev, m_new)]
        pr = [jnp.exp(s - mn) for s, mn in zip(ss, m_new)]
        pv = [_dot(x.astype(BF16), v) for x, v in zip(pr, vs)]
        for h in heads:
            l_ref[h] = alpha[h] * l_ref[h] + jnp.sum(pr[h], axis=-1, keepdims=True)
            acc_ref[h] = alpha[h] * acc_ref[h] + pv[h]
            m_ref[h] = m_new[h]

    def head_rows(page_refs, h):
        return jnp.concatenate([r[pl.ds(h, PAGE, stride=H_B), :].astype(BF16) for r in page_refs], axis=0)

    @pl.when(p < n_steps)
    def _():
        ks = [head_rows(kp_refs, h) for h in heads]
        ss = [_dot_nt(qb_ref[h].astype(BF16), ks[h]) for h in heads]
        update_all(ss, [head_rows(vp_refs, h) for h in heads])

    @pl.when(p == n_steps)
    def _():
        ss, vs = [], []
        for h in heads:
            kn = kn_ref[pl.ds(h, SROWS, stride=H_B), :].astype(BF16)
            s = _dot_nt(qb_ref[h].astype(BF16), kn)
            tq = lax.broadcasted_iota(jnp.int32, s.shape, 0) % SROWS
            tk = lax.broadcasted_iota(jnp.int32, s.shape, 1)
            ss.append(jnp.where(tk <= tq, s, -jnp.inf))
            vs.append(vn_ref[pl.ds(h, SROWS, stride=H_B), :].astype(BF16))
        update_all(ss, vs)
        lam = _lambda_full(lq1_ref[...], lk1_ref[...], lq2_ref[...], lk2_ref[...], lam_init)
        outs = []
        for h in heads:
            an = acc_ref[h] / l_ref[h]
            o = an[0:SROWS] - lam * an[SROWS:N_QROWS]
            outs.append(_subln_gate(o, sub_ref[...], bg_ref[:, h * DV_B:(h + 1) * DV_B], lam_init))
        o_ref[...] = jnp.concatenate(outs, axis=-1)


def _attn_sample(page_table, q, cache_k, cache_v, kn, vn, bg, sub, lq1, lk1, lq2, lk2, lam_init,
                 layer, npg):
    nb, n_pages = page_table.shape
    n_steps = n_pages // npg
    seq = lambda b, p, pt: (b, 0)

    def page(i):
        return lambda b, p, pt: (layer, pt[b, jnp.minimum(p, n_steps - 1) * npg + i], 0, 0)

    full = lambda shp: pl.BlockSpec(shp, lambda b, p, pt: (0,) * len(shp))
    pages = [pl.BlockSpec((None, None, PAGE * H_B, DV_B), page(i)) for i in range(npg)]
    tok = pl.BlockSpec((SROWS, W_B), seq)
    new = pl.BlockSpec((SROWS * H_B, DV_B), seq)
    grid_spec = pltpu.PrefetchScalarGridSpec(
        num_scalar_prefetch=1,
        grid=(nb, n_steps + 1),
        in_specs=[tok] + pages + pages + [new, new, tok, full((1, DV_B)), full((1, DK_B)),
                                          full((1, DK_B)), full((1, DK_B)), full((1, DK_B))],
        out_specs=tok,
        scratch_shapes=[pltpu.VMEM((H_B, N_QROWS, DV_B), F32), pltpu.VMEM((H_B, N_QROWS, 1), F32),
                        pltpu.VMEM((H_B, N_QROWS, 1), F32), pltpu.VMEM((H_B, N_QROWS, DV_B), F32)],
    )
    return pl.pallas_call(
        functools.partial(_attn_sample_kernel, lam_init=lam_init, n_steps=n_steps, npg=npg),
        grid_spec=grid_spec,
        out_shape=jax.ShapeDtypeStruct((nb * SROWS, W_B), F32),
        compiler_params=_cparams(("parallel", "arbitrary")),
        name="attn_sample",
    )(page_table, q, *([cache_k] * npg), *([cache_v] * npg), kn, vn, bg, sub, lq1, lk1, lq2, lk2)


def _split(x):
    hi = x.astype(BF16)
    return hi, (x - hi.astype(F32)).astype(BF16)


def _dot3(a, b):
    ah, al = a
    bh, bl = b
    return _dot(ah, bh) + (_dot(ah, bl) + _dot(al, bh))


def _halves(x, cut):
    lane = lax.broadcasted_iota(jnp.int32, x.shape, x.ndim - 1)
    z = jnp.zeros_like(x)
    return jnp.where(lane < cut, x, z), jnp.where(lane >= cut, x, z)


def _blockdiag(x, cut):
    return jnp.concatenate(_halves(x, cut), axis=0)


def _blockdiag_split(xs, cut):
    return _blockdiag(xs[0], cut), _blockdiag(xs[1], cut)


def _pair_lanes(x0, x1, width, cut):
    lane = lax.broadcasted_iota(jnp.int32, (x0.shape[0], width), 1)
    return jnp.where(lane < cut, x0, x1)


def _tri_inverse_many(a_list, c):
    row = lax.broadcasted_iota(jnp.int32, (c, 2 * c), 0)
    lane = lax.broadcasted_iota(jnp.int32, (c, 2 * c), 1)
    col = jnp.where(lane >= c, lane - c, lane)
    eye = (row == col).astype(F32)
    nb = min(c, 16)
    bd = lambda xs: _blockdiag_split(xs, c)
    ad = [jnp.where(row // nb == col // nb, a, 0.0) for a in a_list]
    t = [eye - x for x in ad]
    pw = ad
    size = 2
    while size < nb:
        ps = _each(_split, pw)
        pw = _each(_dot3, ps, _each(bd, ps))
        tp = _each(_dot3, _each(_split, t), _each(bd, _each(_split, pw)))
        t = _each(lambda x, y: x + y, t, tp)
        size *= 2
    size = nb * 2
    while size <= c:
        off = jnp.logical_and(row // size == col // size, row // (size // 2) != col // (size // 2))
        ts = _each(_split, t)
        tb = _each(bd, ts)
        inner = _each(_dot3, [_split(jnp.where(off, a, 0.0)) for a in a_list], tb)
        outer = _each(_dot3, ts, _each(bd, _each(_split, inner)))
        t = _each(lambda x, y: x - y, t, outer)
        size *= 2
    return t


def _delta_prep_many(items, c):
    dh = DK_C
    row = lax.broadcasted_iota(jnp.int32, (c, 2 * c), 0)
    lane = lax.broadcasted_iota(jnp.int32, (c, 2 * c), 1)
    col = jnp.where(lane >= c, lane - c, lane)
    decay = [jnp.exp(jnp.where(row >= col, it[3] - it[5], -jnp.inf)) for it in items]
    kb = [it[1].astype(BF16) for it in items]
    kbd = [_blockdiag(x, dh) for x in kb]
    kk = _each(_dot_nt, kb, kbd)
    a = [jnp.where(row > col, it[6] * x * d, 0.0) for it, x, d in zip(items, kk, decay)]
    t = _tri_inverse_many(a, c)
    eg = [jnp.exp(it[4]) for it in items]
    rhs = []
    for it, e in zip(items, eg):
        vb = _split(it[2] * it[7])
        kbe = _split(it[1] * (it[7] * e))
        rhs.append(tuple(jnp.concatenate([_blockdiag(x, dh), _blockdiag(y, dh)], axis=-1)
                         for x, y in zip(vb, kbe)))
    uw = _each(_dot3, _each(_split, t), rhs)
    qk = _each(_dot_nt, [it[0].astype(BF16) for it in items], kbd)
    out = []
    for it, x, e, s, d in zip(items, uw, eg, qk, decay):
        q, k, gf = it[0], it[1], it[4]
        glast = gf[c - 1:c, :]
        wq = jnp.concatenate([x[:, 2 * dh:], q * e], axis=0).astype(BF16)
        kt = (k * jnp.exp(glast - gf)).astype(BF16)
        out.append((x[:, :2 * dh], wq, (s * d).astype(BF16), kt, jnp.exp(glast)))
    return out


def _delta_scan(u, wq, qk, kt, gl, s, c):
    r = _dot(wq, _blockdiag(s.astype(BF16), DV_C))
    vnb = (u - r[:c]).astype(BF16)
    o = r[c:] + _dot(qk, _blockdiag(vnb, DV_C))
    full = _dot_tn(kt, vnb)
    lane = lax.broadcasted_iota(jnp.int32, s.shape, 1)
    return o, s * gl + jnp.where(lane < DV_C, full[:DK_C], full[DK_C:])


def _pair_norm(x, cut):
    lo, hi = _halves(x * x, cut)
    lane = lax.broadcasted_iota(jnp.int32, x.shape, 1)
    return jnp.where(lane < cut, jnp.sum(lo, axis=-1, keepdims=True),
                     jnp.sum(hi, axis=-1, keepdims=True))


N_PAIRS = H_C // 2


def _gdn_block(x, cm, alog, dtb, onw, states, c, nseg, chained, valid):
    rows = nseg * c
    pw_ = 2 * DK_C
    beta = _sigmoid(cm[:, W_C:W_C + LANES])
    g = -jnp.exp(alog) * _softplus(cm[:, W_C + LANES:W_C + 2 * LANES] + dtb)
    if valid is not None:
        beta = jnp.where(valid, beta, 0.0)
        g = jnp.where(valid, g, 0.0)
        x = jnp.where(valid, x, 0.0)
    row = lax.broadcasted_iota(jnp.int32, (rows, rows), 0)
    col = lax.broadcasted_iota(jnp.int32, (rows, rows), 1)
    same = row // c == col // c
    lower = jnp.logical_and(same, row >= col).astype(F32).astype(BF16)
    upper = jnp.logical_and(same, row <= col).astype(F32).astype(BF16)
    g1 = g.astype(BF16)
    r1 = g - g1.astype(F32)
    g2 = r1.astype(BF16)
    g3 = (r1 - g2.astype(F32)).astype(BF16)
    gcs = _dot(lower, g1) + (_dot(lower, g2) + _dot(lower, g3))
    gcs_t = _dot_tn(g1, upper) + (_dot_tn(g2, upper) + _dot_tn(g3, upper))
    qs, ks = [], []
    for p in range(N_PAIRS):
        xq = x[:, p * pw_:(p + 1) * pw_]
        xk = x[:, W_C + p * pw_:W_C + (p + 1) * pw_]
        qs.append(xq * lax.rsqrt(_pair_norm(xq, DK_C) + EPS) * (DK_C ** -0.5))
        ks.append(xk * lax.rsqrt(_pair_norm(xk, DK_C) + EPS))
    keys = [(i, p) for i in range(nseg) for p in range(N_PAIRS)]
    seg = lambda a, i: a[i * c:(i + 1) * c]
    items = []
    for i, p in keys:
        h0, h1 = 2 * p, 2 * p + 1
        g0, g1 = seg(gcs, i)[:, h0:h0 + 1], seg(gcs, i)[:, h1:h1 + 1]
        b0, b1 = seg(beta, i)[:, h0:h0 + 1], seg(beta, i)[:, h1:h1 + 1]
        grow = jnp.concatenate([gcs_t[h0:h0 + 1, i * c:(i + 1) * c],
                                gcs_t[h1:h1 + 1, i * c:(i + 1) * c]], axis=-1)
        items.append((seg(qs[p], i), seg(ks[p], i),
                      seg(x, i)[:, 2 * W_C + p * pw_:2 * W_C + (p + 1) * pw_],
                      _pair_lanes(g0, g1, 2 * c, c), _pair_lanes(g0, g1, pw_, DK_C), grow,
                      _pair_lanes(b0, b1, 2 * c, c), _pair_lanes(b0, b1, pw_, DK_C)))
    preps = dict(zip(keys, _delta_prep_many(items, c)))
    onw2 = jnp.concatenate([onw, onw], axis=-1)
    s_cur = list(states)
    ys = []
    for i in range(nseg):
        outs = []
        for p in range(N_PAIRS):
            si = p if chained else i * N_PAIRS + p
            o, s_cur[si] = _delta_scan(*preps[i, p], s_cur[si], c)
            o = o * lax.rsqrt(_pair_norm(o, DV_C) * (1.0 / DV_C) + EPS) * onw2
            outs.append(o * _silu(seg(cm, i)[:, p * pw_:(p + 1) * pw_]))
        ys.append(jnp.concatenate(outs, axis=-1))
    return jnp.concatenate(ys, axis=0), s_cur


def _pack_states(per_head):
    return [jnp.concatenate([per_head[2 * p], per_head[2 * p + 1]], axis=-1) for p in range(N_PAIRS)]


def _unpack_state(packed, h):
    return packed[h // 2][:, (h % 2) * DV_C:(h % 2 + 1) * DV_C]


CONVC_PRE = SUBLANES
CONVC_OFF = CONVC_PRE - (SHORT_CONV - 1)
GDN_CHUNKS = 4


def _gdn_prompt_kernel(x_ref, cm_ref, cw_ref, alog_ref, dtb_ref, onw_ref, y_ref, s_ref,
                       ext_ref, st_ref, *, nch):
    i = pl.program_id(1)
    rows = nch * CHUNK

    @pl.when(i == 0)
    def _():
        ext_ref[0:CONVC_PRE, :] = jnp.zeros((CONVC_PRE, QKV_C), F32)
        st_ref[...] = jnp.zeros(st_ref.shape, F32)

    ext_ref[CONVC_PRE:CONVC_PRE + rows, :] = x_ref[...]
    acc = jnp.zeros((rows, QKV_C), F32)
    for j in range(SHORT_CONV):
        acc = acc + cw_ref[j:j + 1, :] * ext_ref[CONVC_OFF + j:CONVC_OFF + j + rows, :]
    ext_ref[0:CONVC_PRE, :] = ext_ref[rows:rows + CONVC_PRE, :]
    y, s_new = _gdn_block(_silu(acc), cm_ref[...], alog_ref[...], dtb_ref[...], onw_ref[...],
                          [st_ref[p] for p in range(N_PAIRS)], CHUNK, nch, True, None)
    y_ref[...] = y.astype(BF16)
    for p in range(N_PAIRS):
        st_ref[p] = s_new[p]

    @pl.when(i == pl.num_programs(1) - 1)
    def _():
        for h in range(H_C):
            s_ref[h] = _unpack_state(s_new, h)


def _gdn_prompt(x, cm, cw, alog, dtb, onw, bsz, seq, nch):
    rows = nch * CHUNK
    nc = seq // rows
    full = lambda shp: pl.BlockSpec(shp, lambda b, i: (0,) * len(shp))
    tok = lambda wd: pl.BlockSpec((rows, wd), lambda b, i: (b * nc + i, 0))
    return pl.pallas_call(
        functools.partial(_gdn_prompt_kernel, nch=nch),
        grid=(bsz, nc),
        in_specs=[tok(QKV_C), tok(W_CM), full((SHORT_CONV, QKV_C)), full((1, LANES)),
                  full((1, LANES)), full((1, DV_C))],
        out_specs=[tok(W_C), pl.BlockSpec((None, H_C, DK_C, DV_C), lambda b, i: (b, 0, 0, 0))],
        out_shape=[jax.ShapeDtypeStruct((bsz * seq, W_C), BF16),
                   jax.ShapeDtypeStruct((bsz, H_C, DK_C, DV_C), F32)],
        scratch_shapes=[pltpu.VMEM((CONVC_PRE + rows, QKV_C), F32),
                        pltpu.VMEM((N_PAIRS, DK_C, 2 * DV_C), F32)],
        compiler_params=_cparams(("parallel", "arbitrary")),
        name="gdn_prompt",
    )(x, cm, cw, alog, dtb, onw)


def _gdn_sample_kernel(x_ref, st_ref, cm_ref, s0_ref, cw_ref, alog_ref, dtb_ref, onw_ref,
                       y_ref, s_ref, *, n_real):
    accs = []
    for g in range(SEQ_GROUP):
        ext = jnp.concatenate([st_ref[g], x_ref[g * SROWS:(g + 1) * SROWS, :]], axis=0)
        acc = jnp.zeros((SROWS, QKV_C), F32)
        for j in range(SHORT_CONV):
            acc = acc + cw_ref[j:j + 1, :] * ext[CONVC_OFF + j:CONVC_OFF + j + SROWS, :]
        accs.append(acc)
    rows = SEQ_GROUP * SROWS
    valid = lax.broadcasted_iota(jnp.int32, (rows, 1), 0) % SROWS < n_real
    states = [sp for g in range(SEQ_GROUP)
              for sp in _pack_states([s0_ref[g, h] for h in range(H_C)])]
    y, s_new = _gdn_block(_silu(jnp.concatenate(accs, axis=0)), cm_ref[...], alog_ref[...],
                          dtb_ref[...], onw_ref[...], states, SROWS, SEQ_GROUP, False, valid)
    y_ref[...] = y
    for g in range(SEQ_GROUP):
        for h in range(H_C):
            s_ref[g, h] = _unpack_state(s_new[g * N_PAIRS:(g + 1) * N_PAIRS], h)


def _gdn_sample(x, st, cm, s0, cw, alog, dtb, onw, n_real):
    nb = s0.shape[0]
    rows = SEQ_GROUP * SROWS
    full = lambda shp: pl.BlockSpec(shp, lambda g: (0,) * len(shp))
    tok = lambda wd: pl.BlockSpec((rows, wd), lambda g: (g, 0))
    state = pl.BlockSpec((SEQ_GROUP, H_C, DK_C, DV_C), lambda g: (g, 0, 0, 0))
    return pl.pallas_call(
        functools.partial(_gdn_sample_kernel, n_real=n_real),
        grid=(nb // SEQ_GROUP,),
        in_specs=[tok(QKV_C), pl.BlockSpec((SEQ_GROUP, CONVC_PRE, QKV_C), lambda g: (g, 0, 0)),
                  tok(W_CM), state, full((SHORT_CONV, QKV_C)), full((1, LANES)), full((1, LANES)),
                  full((1, DV_C))],
        out_specs=[tok(W_C), state],
        out_shape=[jax.ShapeDtypeStruct((nb * SROWS, W_C), F32),
                   jax.ShapeDtypeStruct((nb, H_C, DK_C, DV_C), F32)],
        compiler_params=_cparams(("parallel",)),
        name="gdn_sample",
    )(x, st, cm, s0, cw, alog, dtb, onw)


def _prepare_w_in(w_in):
    depth, d, _ = w_in.shape
    beta = w_in[:, :, C_CG:C_CG + H_C]
    alpha = w_in[:, :, C_CG + H_C:C_CG + 2 * H_C]
    gate = w_in[:, :, C_CG + 2 * H_C:]
    zpad = jnp.zeros((depth, d, LANES - H_C), w_in.dtype)
    wtail = jnp.concatenate([gate, beta, zpad, alpha, zpad], axis=-1).astype(BF16)
    wt = jnp.concatenate([w_in[:, :, C_Q:C_K], w_in[:, :, C_V:C_BG]], axis=-1)
    return w_in.astype(BF16), wtail, jnp.swapaxes(wt, 1, 2).astype(BF16)


def _pad_lanes(x):
    return jnp.pad(x, ((0, 0), (0, LANES - x.shape[-1])))


def kernel(x_prompt, x_sample, cache_k, cache_v, state_conv_a, state_conv_c, state_delta, page_table,
           norm_w, w_in, conv_a_w, conv_a_b, ln_a_w, ln_a_b, pw_a_w, pw_a_b,
           lambda_q1, lambda_k1, lambda_q2, lambda_k2, subln_w, conv_c_w, a_log, dt_bias,
           onorm_c_w, w_out, final_norm_w):
    bsz, seq, d = x_prompt.shape
    nb, ns, _ = x_sample.shape
    depth = w_in.shape[0]
    n_pool = cache_k.shape[1]
    n_pages = page_table.shape[1]
    t_attn = ATTN_TILE if seq % ATTN_TILE == 0 else seq
    tl_conv = 512 if seq % 512 == 0 else seq
    tm_p = 256
    tm_o = 512 if (bsz * seq) % 512 == 0 else tm_p
    nch = GDN_CHUNKS if seq % (GDN_CHUNKS * CHUNK) == 0 else 1
    npg = PAGES_PER_STEP if n_pages % PAGES_PER_STEP == 0 else 1
    assert SHORT_CONV - 1 <= ns <= SROWS, "sample tokens must fit one 8-row tile"
    assert nb % SEQ_GROUP == 0 and seq % CHUNK == 0 and seq % t_attn == 0 and (bsz * seq) % tm_p == 0
    assert tl_conv % CONV_RC == 0 and tl_conv >= CONV_PRE

    wp, wtail, wt = _prepare_w_in(w_in)
    wo = w_out.astype(BF16)
    pw = pw_a_w.astype(BF16)
    ck = cache_k.reshape(depth, n_pool, PAGE * H_B, DV_B)
    cv = cache_v.reshape(depth, n_pool, PAGE * H_B, DV_B)
    st_a = jnp.pad(state_conv_a, ((0, 0), (0, 0), (CONV_PRE - (CONV_A - 1), 0), (0, 0)))
    st_c = jnp.pad(state_conv_c, ((0, 0), (0, 0), (CONVC_PRE - (SHORT_CONV - 1), 0), (0, 0)))
    alog = _pad_lanes(a_log)
    dtb = _pad_lanes(dt_bias)
    fw = final_norm_w.reshape(1, d)

    hp = x_prompt.reshape(bsz * seq, d)
    hs = jnp.pad(x_sample, ((0, 0), (0, SROWS - ns), (0, 0))).reshape(nb * SROWS, d)
    outs_p = [[] for _ in range(5)]
    outs_s = [[] for _ in range(5)]
    for l in range(depth):
        lam_init = 0.8 - 0.6 * math.exp(-0.3 * l)
        final = l == depth - 1
        row = lambda a: a[l].reshape(1, -1)
        lams = (row(lambda_q1), row(lambda_k1), row(lambda_q2), row(lambda_k2))

        a3, qt, k32, k16, v32, vt, bg, qkvc, cm = _proj(hp, row(norm_w), wp, wtail, wt, l, tm_p, True)
        ya, tail = _conva_prompt(a3, conv_a_w[l], row(conv_a_b), row(ln_a_w), row(ln_a_b), pw[l],
                                 row(pw_a_b), bsz, seq, tl_conv)
        yb = _attn_prompt(qt, k16, vt, bg, row(subln_w), *lams, lam_init, bsz, seq, t_attn)
        yc, s_new = _gdn_prompt(qkvc, cm, conv_c_w[l], alog[l:l + 1], dtb[l:l + 1], row(onorm_c_w),
                                bsz, seq, nch)
        hp = _outproj(ya, yb, yc, hp, wo[l], fw, tm_o, final)
        outs_p[0].append(k32.reshape(bsz, seq, H_B, 2 * DK_B))
        outs_p[1].append(v32.reshape(bsz, seq, H_B, DV_B))
        outs_p[2].append(tail[:, CONV_PRE - (CONV_A - 1):])
        outs_p[3].append(qkvc.reshape(bsz, seq, QKV_C)[:, seq - (SHORT_CONV - 1):])
        outs_p[4].append(s_new)

        a3, q, k32, v32, bg, qkvc, cm = _proj(hs, row(norm_w), wp, wtail, wt, l, nb * SROWS, False)
        ya, u = _conva_sample(a3.reshape(nb, SROWS, 3 * W_A), st_a[l], conv_a_w[l], row(conv_a_b),
                              row(ln_a_w), row(ln_a_b), pw[l], row(pw_a_b))
        yb = _attn_sample(page_table, q, ck, cv, k32, v32, bg, row(subln_w), *lams, lam_init, l, npg)
        yc, s_new = _gdn_sample(qkvc, st_c[l], cm, state_delta[l], conv_c_w[l], alog[l:l + 1],
                                dtb[l:l + 1], row(onorm_c_w), ns)
        hs = _outproj(ya.reshape(nb * SROWS, W_A), yb, yc, hs, wo[l], fw, nb * SROWS, final)
        outs_s[0].append(k32.reshape(nb, SROWS, H_B, 2 * DK_B)[:, :ns])
        outs_s[1].append(v32.reshape(nb, SROWS, H_B, DV_B)[:, :ns])
        outs_s[2].append(jnp.concatenate([state_conv_a[l], u[:, :ns]], axis=1)[:, -(CONV_A - 1):])
        outs_s[3].append(jnp.concatenate([state_conv_c[l], qkvc.reshape(nb, SROWS, QKV_C)[:, :ns]],
                                         axis=1)[:, -(SHORT_CONV - 1):])
        outs_s[4].append(s_new)

    y_prompt = hp.reshape(bsz, seq, d)
    y_sample = hs.reshape(nb, SROWS, d)[:, :ns]
    stack = lambda xs: jnp.stack(xs)
    return (y_prompt, y_sample,
            stack(outs_p[0]), stack(outs_p[1]), stack(outs_p[2]), stack(outs_p[3]),
            stack(outs_p[4]).astype(state_delta.dtype),
            stack(outs_s[0]), stack(outs_s[1]), stack(outs_s[2]), stack(outs_s[3]),
            stack(outs_s[4]).astype(state_delta.dtype))
```

```python
import functools
import math

import jax
import jax.numpy as jnp
from jax import lax
from jax.experimental import pallas as pl
from jax.experimental.pallas import tpu as pltpu

F32 = jnp.float32
BF16 = jnp.bfloat16

EPS = 1e-6
LOG2E = math.log2(math.e)
W_A = 256
CONV_A = 31
H_B = 4
DK_B = 64
DV_B = 128
W_B = H_B * DV_B
H_C = 4
DK_C = 64
DV_C = 64
W_C = H_C * DV_C
QKV_C = H_C * (2 * DK_C + DV_C)
SHORT_CONV = 4
CHUNK = 64
PAGE = 128
D_MIX = W_A + W_B + W_C

LANES = 128
SUBLANES = 8
SROWS = SUBLANES
VMEM_LIMIT = 48 * 1024 * 1024

C_A3 = 0
C_Q = C_A3 + 3 * W_A
C_K = C_Q + W_B
C_V = C_K + W_B
C_BG = C_V + W_B
C_QKVC = C_BG + W_B
C_CG = C_QKVC + QKV_C
W_CM = W_C + 2 * LANES


def _cparams(sem):
    return pltpu.CompilerParams(dimension_semantics=sem, vmem_limit_bytes=VMEM_LIMIT)


def _sigmoid(x):
    return 1.0 / (1.0 + jnp.exp(-x))


def _silu(x):
    return x * _sigmoid(x)


def _softplus(x):
    return jnp.maximum(x, 0.0) + jnp.log1p(jnp.exp(-jnp.abs(x)))


def _dot(a, b):
    return jnp.dot(a, b, preferred_element_type=F32)


def _dot_nt(a, b):
    return lax.dot_general(a, b, (((1,), (1,)), ((), ())), preferred_element_type=F32)


def _dot_tn(a, b):
    return lax.dot_general(a, b, (((0,), (0,)), ((), ())), preferred_element_type=F32)


def _each(f, *lists):
    return [f(*xs) for xs in zip(*lists)]


def _proj_kernel(x_ref, nw_ref, w_ref, wtail_ref, wt_ref, *refs, prompt):
    out_refs = refs[2:] if prompt else refs
    x = x_ref[...]
    ms = jnp.mean(x * x, axis=-1, keepdims=True)
    hn = (x * lax.rsqrt(ms + EPS) * nw_ref[...]).astype(BF16)

    def mm(lo, hi):
        return _dot(hn, w_ref[:, lo:hi])

    if prompt:
        a3_ref, qt_ref, k32_ref, k16_ref, v32_ref, vt_ref, bg_ref, qkvc_ref, cm_ref = out_refs
        qt_ref[...] = (_dot_nt(wt_ref[0:W_B, :], hn) * (LOG2E * DK_B ** -0.5)).astype(BF16)
        vt = _dot_nt(wt_ref[W_B:2 * W_B, :], hn)
        vt_ref[...] = vt.astype(BF16)
        v = vt.T
        k = mm(C_K, C_V)
        k16_ref[...] = k.astype(BF16)
    else:
        a3_ref, q_ref, k32_ref, v32_ref, bg_ref, qkvc_ref, cm_ref = out_refs
        q_ref[...] = mm(C_Q, C_K) * (DK_B ** -0.5)
        k = mm(C_K, C_V)
        v = mm(C_V, C_BG)
    a3_ref[...] = mm(C_A3, C_Q)
    tm = x.shape[0]
    for h in range(H_B):
        k32_ref[pl.ds(h, tm, stride=H_B), :] = k[:, h * DV_B:(h + 1) * DV_B]
        v32_ref[pl.ds(h, tm, stride=H_B), :] = v[:, h * DV_B:(h + 1) * DV_B]
    bg_ref[...] = mm(C_BG, C_QKVC)
    qkvc_ref[...] = mm(C_QKVC, C_CG)
    cm_ref[...] = _dot(hn, wtail_ref[...])


def _proj(x, nw, w, wtail, wt, layer, tm, kv_all=None):
    m, d = x.shape
    n_in = w.shape[-1]
    prompt = kv_all is not None
    per_layer = lambda r, c: pl.BlockSpec((None, r, c), lambda i: (layer, 0, 0))
    tok = lambda wd, dt: (pl.BlockSpec((tm, wd), lambda i: (i, 0)), jax.ShapeDtypeStruct((m, wd), dt))
    tr = lambda: (pl.BlockSpec((W_B, tm), lambda i: (0, i)), jax.ShapeDtypeStruct((W_B, m), BF16))
    hd = lambda: (pl.BlockSpec((tm * H_B, DV_B), lambda i: (i, 0)),
                  jax.ShapeDtypeStruct((m * H_B, DV_B), F32))
    hd_all = lambda a: (pl.BlockSpec((None, tm * H_B, DV_B), lambda i: (layer, i, 0)),
                        jax.ShapeDtypeStruct(a.shape, a.dtype))
    in_specs = [pl.BlockSpec((tm, d), lambda i: (i, 0)),
                pl.BlockSpec((1, d), lambda i: (0, 0)),
                per_layer(d, n_in), per_layer(d, W_CM), per_layer(2 * W_B, d)]
    args = [x, nw, w, wtail, wt]
    aliases = {}
    if prompt:
        outs = [tok(3 * W_A, F32), tr(), hd_all(kv_all[0]), tok(W_B, BF16), hd_all(kv_all[1]), tr(),
                tok(W_B, F32), tok(QKV_C, F32), tok(W_CM, F32)]
        in_specs += [pl.BlockSpec(memory_space=pl.ANY)] * 2
        aliases = {len(args): 2, len(args) + 1: 4}
        args += list(kv_all)
    else:
        outs = [tok(3 * W_A, F32), tok(W_B, F32), hd(), hd(),
                tok(W_B, F32), tok(QKV_C, F32), tok(W_CM, F32)]
    return pl.pallas_call(
        functools.partial(_proj_kernel, prompt=prompt),
        grid=(m // tm,),
        in_specs=in_specs,
        out_specs=[o[0] for o in outs],
        out_shape=[o[1] for o in outs],
        input_output_aliases=aliases,
        compiler_params=_cparams(("parallel",)),
        name="proj",
    )(*args)


def _outproj_kernel(ya_ref, yb_ref, yc_ref, h_ref, w_ref, fw_ref, o_ref, *, final):
    ycat = jnp.concatenate([ya_ref[...].astype(BF16), yb_ref[...].astype(BF16),
                            yc_ref[...].astype(BF16)], axis=-1)
    hnew = h_ref[...] + _dot(ycat, w_ref[...])
    if final:
        ms = jnp.mean(hnew * hnew, axis=-1, keepdims=True)
        hnew = hnew * lax.rsqrt(ms + EPS) * fw_ref[...]
    o_ref[...] = hnew


def _outproj(ya, yb, yc, h, w, fw, tm, final):
    m, d = h.shape
    row = lambda wd: pl.BlockSpec((tm, wd), lambda i: (i, 0))
    return pl.pallas_call(
        functools.partial(_outproj_kernel, final=final),
        grid=(m // tm,),
        in_specs=[row(W_A), row(W_B), row(W_C), row(d),
                  pl.BlockSpec((D_MIX, d), lambda i: (0, 0)),
                  pl.BlockSpec((1, d), lambda i: (0, 0))],
        out_specs=row(d),
        out_shape=jax.ShapeDtypeStruct((m, d), F32),
        compiler_params=_cparams(("parallel",)),
        name="outproj",
    )(ya, yb, yc, h, w, fw)


def _conva_tail(conv, a_gate, cb, lnw, lnb, pw, pb):
    y = conv + cb
    mu = jnp.mean(y, axis=-1, keepdims=True)
    yc = y - mu
    y = yc * lax.rsqrt(jnp.mean(yc * yc, axis=-1, keepdims=True) + EPS) * lnw + lnb
    y = _silu(y)
    y = _dot(y.astype(BF16), pw) + pb
    return y * _silu(a_gate)


CONV_PRE = 32
CONV_OFF = CONV_PRE - (CONV_A - 1)
CONV_RC = 64


def _conva_prompt_kernel(a3_ref, cw_ref, cb_ref, lnw_ref, lnb_ref, pw_ref, pb_ref,
                         ya_ref, tail_ref, ext_ref, *, tl):
    i = pl.program_id(1)

    @pl.when(i == 0)
    def _():
        ext_ref[0, 0:CONV_PRE, :] = jnp.zeros((CONV_PRE, W_A), F32)

    u = a3_ref[:, 0:W_A] * _sigmoid(a3_ref[:, W_A:2 * W_A])
    ext_ref[0, CONV_PRE:CONV_PRE + tl, :] = u
    n_shift = CONV_PRE + tl - SUBLANES
    for s in range(1, SUBLANES):
        ext_ref[s, 0:n_shift, :] = ext_ref[0, s:s + n_shift, :]
    for r in range(tl // CONV_RC):
        acc = jnp.zeros((CONV_RC, W_A), F32)
        for j in range(CONV_A):
            off = r * CONV_RC + CONV_OFF + j
            s = off % SUBLANES
            acc = acc + cw_ref[j:j + 1, :] * ext_ref[s, off - s:off - s + CONV_RC, :]
        gate = a3_ref[r * CONV_RC:(r + 1) * CONV_RC, 2 * W_A:3 * W_A]
        y = _conva_tail(acc, gate, cb_ref[...], lnw_ref[...], lnb_ref[...], pw_ref[...], pb_ref[...])
        ya_ref[r * CONV_RC:(r + 1) * CONV_RC, :] = y.astype(BF16)
    last = ext_ref[0, tl:tl + CONV_PRE, :]
    tail_ref[...] = last
    ext_ref[0, 0:CONV_PRE, :] = last


def _conva_prompt(a3, cw, cb, lnw, lnb, pw, pb, bsz, seq, tl):
    nl = seq // tl
    full = lambda shp: pl.BlockSpec(shp, lambda b, i: (0,) * len(shp))
    return pl.pallas_call(
        functools.partial(_conva_prompt_kernel, tl=tl),
        grid=(bsz, nl),
        in_specs=[pl.BlockSpec((tl, 3 * W_A), lambda b, i: (b * nl + i, 0)),
                  full((CONV_A, W_A)), full((1, W_A)), full((1, W_A)), full((1, W_A)),
                  full((W_A, W_A)), full((1, W_A))],
        out_specs=[pl.BlockSpec((tl, W_A), lambda b, i: (b * nl + i, 0)),
                   pl.BlockSpec((None, CONV_PRE, W_A), lambda b, i: (b, 0, 0))],
        out_shape=[jax.ShapeDtypeStruct((bsz * seq, W_A), BF16),
                   jax.ShapeDtypeStruct((bsz, CONV_PRE, W_A), F32)],
        scratch_shapes=[pltpu.VMEM((SUBLANES, CONV_PRE + tl, W_A), F32)],
        compiler_params=_cparams(("parallel", "arbitrary")),
        name="conva_prompt",
    )(a3, cw, cb, lnw, lnb, pw, pb)


SEQ_GROUP = 8


def _conva_sample_kernel(a3_ref, st_ref, cw_ref, cb_ref, lnw_ref, lnb_ref, pw_ref, pb_ref,
                         ya_ref, u_ref, ext_ref):
    a3 = a3_ref[...]
    u = a3[:, :, 0:W_A] * _sigmoid(a3[:, :, W_A:2 * W_A])
    u_ref[...] = u
    ext_ref[:, 0:CONV_PRE, :] = st_ref[...]
    ext_ref[:, CONV_PRE:CONV_PRE + SROWS, :] = u
    acc = jnp.zeros((SEQ_GROUP, SROWS, W_A), F32)
    for j in range(CONV_A):
        acc = acc + cw_ref[j:j + 1, :] * ext_ref[:, CONV_OFF + j:CONV_OFF + j + SROWS, :]
    rows = SEQ_GROUP * SROWS
    y = _conva_tail(acc.reshape(rows, W_A), a3[:, :, 2 * W_A:3 * W_A].reshape(rows, W_A),
                    cb_ref[...], lnw_ref[...], lnb_ref[...], pw_ref[...], pb_ref[...])
    ya_ref[...] = y.reshape(SEQ_GROUP, SROWS, W_A)


def _conva_sample(a3, st, cw, cb, lnw, lnb, pw, pb):
    nb = a3.shape[0]
    full = lambda shp: pl.BlockSpec(shp, lambda g: (0,) * len(shp))
    seqs = lambda r, wd: pl.BlockSpec((SEQ_GROUP, r, wd), lambda g: (g, 0, 0))
    return pl.pallas_call(
        _conva_sample_kernel,
        grid=(nb // SEQ_GROUP,),
        in_specs=[seqs(SROWS, 3 * W_A), seqs(CONV_PRE, W_A),
                  full((CONV_A, W_A)), full((1, W_A)), full((1, W_A)), full((1, W_A)),
                  full((W_A, W_A)), full((1, W_A))],
        out_specs=[seqs(SROWS, W_A), seqs(SROWS, W_A)],
        out_shape=[jax.ShapeDtypeStruct((nb, SROWS, W_A), F32),
                   jax.ShapeDtypeStruct((nb, SROWS, W_A), F32)],
        scratch_shapes=[pltpu.VMEM((SEQ_GROUP, CONV_PRE + SROWS, W_A), F32)],
        compiler_params=_cparams(("parallel",)),
        name="conva_sample",
    )(a3, st, cw, cb, lnw, lnb, pw, pb)


def _lambda_full(lq1, lk1, lq2, lk2, lam_init):
    s1 = jnp.sum(lq1 * lk1, axis=-1, keepdims=True)
    s2 = jnp.sum(lq2 * lk2, axis=-1, keepdims=True)
    return jnp.exp(s1) - jnp.exp(s2) + lam_init


def _subln_gate(o, sub, bg, lam_init):
    y = o * lax.rsqrt(jnp.mean(o * o, axis=-1, keepdims=True) + EPS) * sub
    return (y * (1.0 - lam_init)) * _silu(bg)


ATTN_TILE = 2048
ATTN_STRIP = 256


def _attn_prompt_kernel(qt_ref, k_ref, vt_ref, bg_ref, sub_ref, lq1_ref, lk1_ref, lq2_ref, lk2_ref,
                        o_ref, m1_ref, l1_ref, a1_ref, m2_ref, l2_ref, a2_ref, *, lam_init, t):
    qi = pl.program_id(2)
    qt = qt_ref[...]
    feat = lax.broadcasted_iota(jnp.int32, qt.shape, 0)
    zero = jnp.zeros_like(qt)
    qs = (jnp.where(feat < DK_B, qt, zero), jnp.where(feat >= DK_B, qt, zero))
    stats = ((m1_ref, l1_ref, a1_ref), (m2_ref, l2_ref, a2_ref))
    for m_ref, l_ref, a_ref in stats:
        m_ref[...] = jnp.full(m_ref.shape, -jnp.inf, F32)
        l_ref[...] = jnp.zeros(l_ref.shape, F32)
        a_ref[...] = jnp.zeros(a_ref.shape, F32)

    sw = min(ATTN_STRIP, t)
    units = [(mp, c) for c in range(t // sw) for mp in (0, 1)]

    def tile(j, causal):
        start = pl.multiple_of(j * t, t)
        k = k_ref[pl.ds(start, t), :]
        vt = vt_ref[:, pl.ds(start, t)]
        nkeys = lambda c: (c + 1) * sw if causal else t

        def scores(u):
            mp, c = u
            return _dot(k[:nkeys(c)], qs[mp][:, c * sw:(c + 1) * sw])

        def softmax(u, st):
            mp, c = u
            m_ref, l_ref, _ = stats[mp]
            cols = slice(c * sw, (c + 1) * sw)
            if causal:
                kr = lax.broadcasted_iota(jnp.int32, st.shape, 0)
                qc = lax.broadcasted_iota(jnp.int32, st.shape, 1) + c * sw
                st = jnp.where(kr <= qc, st, -jnp.inf)
            m_prev = m_ref[:, cols]
            m_new = jnp.maximum(m_prev, jnp.max(st, axis=0, keepdims=True))
            alpha = jnp.exp2(m_prev - m_new)
            pt = jnp.exp2(st - m_new)
            l_ref[:, cols] = alpha * l_ref[:, cols] + jnp.sum(pt, axis=0, keepdims=True)
            m_ref[:, cols] = m_new
            return pt.astype(BF16), alpha

        def weighted_values(u, pt, alpha):
            mp, c = u
            a_ref = stats[mp][2]
            cols = slice(c * sw, (c + 1) * sw)
            a_ref[:, cols] = alpha * a_ref[:, cols] + _dot(vt[:, :nkeys(c)], pt)

        n = len(units)
        sts, pts = {}, {}
        for i in range(n + 2):
            if i < n:
                sts[i] = scores(units[i])
            if 0 <= i - 1 < n:
                pts[i - 1] = softmax(units[i - 1], sts.pop(i - 1))
            if 0 <= i - 2 < n:
                weighted_values(units[i - 2], *pts.pop(i - 2))

    def body(j, carry):
        tile(j, False)
        return carry

    lax.fori_loop(0, qi, body, 0)
    tile(qi, True)
    lam = _lambda_full(lq1_ref[...], lk1_ref[...], lq2_ref[...], lk2_ref[...], lam_init)
    ot = a1_ref[...] / l1_ref[...] - lam * (a2_ref[...] / l2_ref[...])
    o_ref[...] = _subln_gate(ot.T, sub_ref[...], bg_ref[...], lam_init).astype(BF16)


def _attn_prompt(qt, k, vt, bg, sub, lq1, lk1, lq2, lk2, lam_init, bsz, seq, t):
    n = seq // t
    full = lambda shp: pl.BlockSpec(shp, lambda b, h, qi: (0,) * len(shp))
    return pl.pallas_call(
        functools.partial(_attn_prompt_kernel, lam_init=lam_init, t=t),
        grid=(bsz, H_B, n),
        in_specs=[pl.BlockSpec((DV_B, t), lambda b, h, qi: (h, b * n + qi)),
                  pl.BlockSpec((seq, DV_B), lambda b, h, qi: (b, h)),
                  pl.BlockSpec((DV_B, seq), lambda b, h, qi: (h, b)),
                  pl.BlockSpec((t, DV_B), lambda b, h, qi: (b * n + qi, h)),
                  full((1, DV_B)), full((1, DK_B)), full((1, DK_B)), full((1, DK_B)),
                  full((1, DK_B))],
        out_specs=pl.BlockSpec((t, DV_B), lambda b, h, qi: (b * n + qi, h)),
        out_shape=jax.ShapeDtypeStruct((bsz * seq, W_B), BF16),
        scratch_shapes=[pltpu.VMEM((1, t), F32), pltpu.VMEM((1, t), F32), pltpu.VMEM((DV_B, t), F32),
                        pltpu.VMEM((1, t), F32), pltpu.VMEM((1, t), F32), pltpu.VMEM((DV_B, t), F32)],
        compiler_params=_cparams(("parallel", "parallel", "arbitrary")),
        name="attn_prompt",
    )(qt, k, vt, bg, sub, lq1, lk1, lq2, lk2)


PAGES_PER_STEP = 16
N_QROWS = 2 * SROWS


def _attn_sample_kernel(pt_ref, q_ref, *refs, lam_init, n_steps, npg):
    kp_refs = refs[:npg]
    vp_refs = refs[npg:2 * npg]
    (kn_ref, vn_ref, bg_ref, sub_ref, lq1_ref, lk1_ref, lq2_ref, lk2_ref, o_ref,
     qb_ref, m_ref, l_ref, acc_ref) = refs[2 * npg:]
    p = pl.program_id(1)
    heads = range(H_B)

    @pl.when(p == 0)
    def _():
        q = q_ref[...]
        lane = lax.broadcasted_iota(jnp.int32, (SROWS, DV_B), 1)
        for h in heads:
            qh = q[:, h * DV_B:(h + 1) * DV_B]
            qb_ref[h, 0:SROWS, :] = jnp.where(lane < DK_B, qh, 0.0)
            qb_ref[h, SROWS:N_QROWS, :] = jnp.where(lane >= DK_B, qh, 0.0)
        m_ref[...] = jnp.full(m_ref.shape, -jnp.inf, F32)
        l_ref[...] = jnp.zeros(l_ref.shape, F32)
        acc_ref[...] = jnp.zeros(acc_ref.shape, F32)

    def update_all(ss, vs):
        m_prev = [m_ref[h] for h in heads]
        m_new = [jnp.maximum(mp, jnp.max(s, axis=-1, keepdims=True)) for mp, s in zip(m_prev, ss)]
        alpha = [jnp.exp(mp - mn) for mp, mn in zip(m_prev, m_new)]
        pr = [jnp.exp(s - mn) for s, mn in zip(ss, m_new)]
        pv = [_dot(x.astype(BF16), v) for x, v in zip(pr, vs)]
        for h in heads:
            l_ref[h] = alpha[h] * l_ref[h] + jnp.sum(pr[h], axis=-1, keepdims=True)
            acc_ref[h] = alpha[h] * acc_ref[h] + pv[h]
            m_ref[h] = m_new[h]

    def head_rows(page_refs, h):
        return jnp.concatenate([r[pl.ds(h, PAGE, stride=H_B), :].astype(BF16) for r in page_refs], axis=0)

    @pl.when(p < n_steps)
    def _():
        ks = [head_rows(kp_refs, h) for h in heads]
        ss = [_dot_nt(qb_ref[h].astype(BF16), ks[h]) for h in heads]
        update_all(ss, [head_rows(vp_refs, h) for h in heads])

    @pl.when(p == n_steps)
    def _():
        ss, vs = [], []
        for h in heads:
            kn = kn_ref[pl.ds(h, SROWS, stride=H_B), :].astype(BF16)
            s = _dot_nt(qb_ref[h].astype(BF16), kn)
            tq = lax.broadcasted_iota(jnp.int32, s.shape, 0) % SROWS
            tk = lax.broadcasted_iota(jnp.int32, s.shape, 1)
            ss.append(jnp.where(tk <= tq, s, -jnp.inf))
            vs.append(vn_ref[pl.ds(h, SROWS, stride=H_B), :].astype(BF16))
        update_all(ss, vs)
        lam = _lambda_full(lq1_ref[...], lk1_ref[...], lq2_ref[...], lk2_ref[...], lam_init)
        outs = []
        for h in heads:
            an = acc_ref[h] / l_ref[h]
            o = an[0:SROWS] - lam * an[SROWS:N_QROWS]
            outs.append(_subln_gate(o, sub_ref[...], bg_ref[:, h * DV_B:(h + 1) * DV_B], lam_init))
        o_ref[...] = jnp.concatenate(outs, axis=-1)


def _attn_sample(page_table, q, cache_k, cache_v, kn, vn, bg, sub, lq1, lk1, lq2, lk2, lam_init,
                 layer, npg):
    nb, n_pages = page_table.shape
    n_steps = n_pages // npg
    seq = lambda b, p, pt: (b, 0)

    def page(i):
        return lambda b, p, pt: (layer, pt[b, jnp.minimum(p, n_steps - 1) * npg + i], 0, 0)

    full = lambda shp: pl.BlockSpec(shp, lambda b, p, pt: (0,) * len(shp))
    pages = [pl.BlockSpec((None, None, PAGE * H_B, DV_B), page(i)) for i in range(npg)]
    tok = pl.BlockSpec((SROWS, W_B), seq)
    new = pl.BlockSpec((SROWS * H_B, DV_B), seq)
    grid_spec = pltpu.PrefetchScalarGridSpec(
        num_scalar_prefetch=1,
        grid=(nb, n_steps + 1),
        in_specs=[tok] + pages + pages + [new, new, tok, full((1, DV_B)), full((1, DK_B)),
                                          full((1, DK_B)), full((1, DK_B)), full((1, DK_B))],
        out_specs=tok,
        scratch_shapes=[pltpu.VMEM((H_B, N_QROWS, DV_B), F32), pltpu.VMEM((H_B, N_QROWS, 1), F32),
                        pltpu.VMEM((H_B, N_QROWS, 1), F32), pltpu.VMEM((H_B, N_QROWS, DV_B), F32)],
    )
    return pl.pallas_call(
        functools.partial(_attn_sample_kernel, lam_init=lam_init, n_steps=n_steps, npg=npg),
        grid_spec=grid_spec,
        out_shape=jax.ShapeDtypeStruct((nb * SROWS, W_B), F32),
        compiler_params=_cparams(("parallel", "arbitrary")),
        name="attn_sample",
    )(page_table, q, *([cache_k] * npg), *([cache_v] * npg), kn, vn, bg, sub, lq1, lk1, lq2, lk2)


def _split(x):
    hi = x.astype(BF16)
    return hi, (x - hi.astype(F32)).astype(BF16)


def _dot3(a, b):
    ah, al = a
    bh, bl = b
    return _dot(ah, bh) + (_dot(ah, bl) + _dot(al, bh))


def _halves(x, cut):
    lane = lax.broadcasted_iota(jnp.int32, x.shape, x.ndim - 1)
    z = jnp.zeros_like(x)
    return jnp.where(lane < cut, x, z), jnp.where(lane >= cut, x, z)


def _blockdiag(x, cut):
    return jnp.concatenate(_halves(x, cut), axis=0)


def _blockdiag_split(xs, cut):
    return _blockdiag(xs[0], cut), _blockdiag(xs[1], cut)


def _pair_lanes(x0, x1, width, cut):
    lane = lax.broadcasted_iota(jnp.int32, (x0.shape[0], width), 1)
    return jnp.where(lane < cut, x0, x1)


def _tri_inverse_many(a_list, c):
    row = lax.broadcasted_iota(jnp.int32, (c, 2 * c), 0)
    lane = lax.broadcasted_iota(jnp.int32, (c, 2 * c), 1)
    col = jnp.where(lane >= c, lane - c, lane)
    eye = (row == col).astype(F32)
    nb = min(c, 16)
    bd = lambda xs: _blockdiag_split(xs, c)
    ad = [jnp.where(row // nb == col // nb, a, 0.0) for a in a_list]
    t = [eye - x for x in ad]
    pw = ad
    size = 2
    while size < nb:
        ps = _each(_split, pw)
        pw = _each(_dot3, ps, _each(bd, ps))
        tp = _each(_dot3, _each(_split, t), _each(bd, _each(_split, pw)))
        t = _each(lambda x, y: x + y, t, tp)
        size *= 2
    size = nb * 2
    while size <= c:
        off = jnp.logical_and(row // size == col // size, row // (size // 2) != col // (size // 2))
        ts = _each(_split, t)
        tb = _each(bd, ts)
        inner = _each(_dot3, [_split(jnp.where(off, a, 0.0)) for a in a_list], tb)
        outer = _each(_dot3, ts, _each(bd, _each(_split, inner)))
        t = _each(lambda x, y: x - y, t, outer)
        size *= 2
    return t


def _delta_prep_many(items, c):
    dh = DK_C
    row = lax.broadcasted_iota(jnp.int32, (c, 2 * c), 0)
    lane = lax.broadcasted_iota(jnp.int32, (c, 2 * c), 1)
    col = jnp.where(lane >= c, lane - c, lane)
    decay = [jnp.exp(jnp.where(row >= col, it[3] - it[5], -jnp.inf)) for it in items]
    kb = [it[1].astype(BF16) for it in items]
    kbd = [_blockdiag(x, dh) for x in kb]
    kk = _each(_dot_nt, kb, kbd)
    a = [jnp.where(row > col, it[6] * x * d, 0.0) for it, x, d in zip(items, kk, decay)]
    t = _tri_inverse_many(a, c)
    eg = [jnp.exp(it[4]) for it in items]
    rhs = []
    for it, e in zip(items, eg):
        vb = _split(it[2] * it[7])
        kbe = _split(it[1] * (it[7] * e))
        rhs.append(tuple(jnp.concatenate([_blockdiag(x, dh), _blockdiag(y, dh)], axis=-1)
                         for x, y in zip(vb, kbe)))
    uw = _each(_dot3, _each(_split, t), rhs)
    qk = _each(_dot_nt, [it[0].astype(BF16) for it in items], kbd)
    out = []
    for it, x, e, s, d in zip(items, uw, eg, qk, decay):
        q, k, gf = it[0], it[1], it[4]
        glast = gf[c - 1:c, :]
        wq = jnp.concatenate([x[:, 2 * dh:], q * e], axis=0).astype(BF16)
        kt = (k * jnp.exp(glast - gf)).astype(BF16)
        out.append((x[:, :2 * dh], wq, (s * d).astype(BF16), kt, jnp.exp(glast)))
    return out


def _delta_scan(u, wq, qk, kt, gl, s, c):
    r = _dot(wq, _blockdiag(s.astype(BF16), DV_C))
    vnb = (u - r[:c]).astype(BF16)
    o = r[c:] + _dot(qk, _blockdiag(vnb, DV_C))
    full = _dot_tn(kt, vnb)
    lane = lax.broadcasted_iota(jnp.int32, s.shape, 1)
    return o, s * gl + jnp.where(lane < DV_C, full[:DK_C], full[DK_C:])


def _pair_norm(x, cut):
    lo, hi = _halves(x * x, cut)
    lane = lax.broadcasted_iota(jnp.int32, x.shape, 1)
    return jnp.where(lane < cut, jnp.sum(lo, axis=-1, keepdims=True),
                     jnp.sum(hi, axis=-1, keepdims=True))


N_PAIRS = H_C // 2


def _gdn_block(x, cm, alog, dtb, onw, states, c, nseg, chained, valid):
    rows = nseg * c
    pw_ = 2 * DK_C
    beta = _sigmoid(cm[:, W_C:W_C + LANES])
    g = -jnp.exp(alog) * _softplus(cm[:, W_C + LANES:W_C + 2 * LANES] + dtb)
    if valid is not None:
        beta = jnp.where(valid, beta, 0.0)
        g = jnp.where(valid, g, 0.0)
        x = jnp.where(valid, x, 0.0)
    row = lax.broadcasted_iota(jnp.int32, (rows, rows), 0)
    col = lax.broadcasted_iota(jnp.int32, (rows, rows), 1)
    same = row // c == col // c
    lower = jnp.logical_and(same, row >= col).astype(F32).astype(BF16)
    upper = jnp.logical_and(same, row <= col).astype(F32).astype(BF16)
    g1 = g.astype(BF16)
    r1 = g - g1.astype(F32)
    g2 = r1.astype(BF16)
    g3 = (r1 - g2.astype(F32)).astype(BF16)
    gcs = _dot(lower, g1) + (_dot(lower, g2) + _dot(lower, g3))
    gcs_t = _dot_tn(g1, upper) + (_dot_tn(g2, upper) + _dot_tn(g3, upper))
    qs, ks = [], []
    for p in range(N_PAIRS):
        xq = x[:, p * pw_:(p + 1) * pw_]
        xk = x[:, W_C + p * pw_:W_C + (p + 1) * pw_]
        qs.append(xq * lax.rsqrt(_pair_norm(xq, DK_C) + EPS) * (DK_C ** -0.5))
        ks.append(xk * lax.rsqrt(_pair_norm(xk, DK_C) + EPS))
    keys = [(i, p) for i in range(nseg) for p in range(N_PAIRS)]
    seg = lambda a, i: a[i * c:(i + 1) * c]
    items = []
    for i, p in keys:
        h0, h1 = 2 * p, 2 * p + 1
        g0, g1 = seg(gcs, i)[:, h0:h0 + 1], seg(gcs, i)[:, h1:h1 + 1]
        b0, b1 = seg(beta, i)[:, h0:h0 + 1], seg(beta, i)[:, h1:h1 + 1]
        grow = jnp.concatenate([gcs_t[h0:h0 + 1, i * c:(i + 1) * c],
                                gcs_t[h1:h1 + 1, i * c:(i + 1) * c]], axis=-1)
        items.append((seg(qs[p], i), seg(ks[p], i),
                      seg(x, i)[:, 2 * W_C + p * pw_:2 * W_C + (p + 1) * pw_],
                      _pair_lanes(g0, g1, 2 * c, c), _pair_lanes(g0, g1, pw_, DK_C), grow,
                      _pair_lanes(b0, b1, 2 * c, c), _pair_lanes(b0, b1, pw_, DK_C)))
    preps = dict(zip(keys, _delta_prep_many(items, c)))
    onw2 = jnp.concatenate([onw, onw], axis=-1)
    s_cur = list(states)
    ys = []
    for i in range(nseg):
        outs = []
        for p in range(N_PAIRS):
            si = p if chained else i * N_PAIRS + p
            o, s_cur[si] = _delta_scan(*preps[i, p], s_cur[si], c)
            o = o * lax.rsqrt(_pair_norm(o, DV_C) * (1.0 / DV_C) + EPS) * onw2
            outs.append(o * _silu(seg(cm, i)[:, p * pw_:(p + 1) * pw_]))
        ys.append(jnp.concatenate(outs, axis=-1))
    return jnp.concatenate(ys, axis=0), s_cur


def _pack_states(per_head):
    return [jnp.concatenate([per_head[2 * p], per_head[2 * p + 1]], axis=-1) for p in range(N_PAIRS)]


def _unpack_state(packed, h):
    return packed[h // 2][:, (h % 2) * DV_C:(h % 2 + 1) * DV_C]


CONVC_PRE = SUBLANES
CONVC_OFF = CONVC_PRE - (SHORT_CONV - 1)
GDN_CHUNKS = 4


def _gdn_prompt_kernel(x_ref, cm_ref, cw_ref, alog_ref, dtb_ref, onw_ref, y_ref, s_ref,
                       ext_ref, st_ref, *, nch):
    i = pl.program_id(1)
    rows = nch * CHUNK

    @pl.when(i == 0)
    def _():
        ext_ref[0:CONVC_PRE, :] = jnp.zeros((CONVC_PRE, QKV_C), F32)
        st_ref[...] = jnp.zeros(st_ref.shape, F32)

    ext_ref[CONVC_PRE:CONVC_PRE + rows, :] = x_ref[...]
    acc = jnp.zeros((rows, QKV_C), F32)
    for j in range(SHORT_CONV):
        acc = acc + cw_ref[j:j + 1, :] * ext_ref[CONVC_OFF + j:CONVC_OFF + j + rows, :]
    ext_ref[0:CONVC_PRE, :] = ext_ref[rows:rows + CONVC_PRE, :]
    y, s_new = _gdn_block(_silu(acc), cm_ref[...], alog_ref[...], dtb_ref[...], onw_ref[...],
                          [st_ref[p] for p in range(N_PAIRS)], CHUNK, nch, True, None)
    y_ref[...] = y.astype(BF16)
    for p in range(N_PAIRS):
        st_ref[p] = s_new[p]

    @pl.when(i == pl.num_programs(1) - 1)
    def _():
        for h in range(H_C):
            s_ref[h] = _unpack_state(s_new, h)


def _gdn_prompt(x, cm, cw, alog, dtb, onw, bsz, seq, nch):
    rows = nch * CHUNK
    nc = seq // rows
    full = lambda shp: pl.BlockSpec(shp, lambda b, i: (0,) * len(shp))
    tok = lambda wd: pl.BlockSpec((rows, wd), lambda b, i: (b * nc + i, 0))
    return pl.pallas_call(
        functools.partial(_gdn_prompt_kernel, nch=nch),
        grid=(bsz, nc),
        in_specs=[tok(QKV_C), tok(W_CM), full((SHORT_CONV, QKV_C)), full((1, LANES)),
                  full((1, LANES)), full((1, DV_C))],
        out_specs=[tok(W_C), pl.BlockSpec((None, H_C, DK_C, DV_C), lambda b, i: (b, 0, 0, 0))],
        out_shape=[jax.ShapeDtypeStruct((bsz * seq, W_C), BF16),
                   jax.ShapeDtypeStruct((bsz, H_C, DK_C, DV_C), F32)],
        scratch_shapes=[pltpu.VMEM((CONVC_PRE + rows, QKV_C), F32),
                        pltpu.VMEM((N_PAIRS, DK_C, 2 * DV_C), F32)],
        compiler_params=_cparams(("parallel", "arbitrary")),
        name="gdn_prompt",
    )(x, cm, cw, alog, dtb, onw)


def _gdn_sample_kernel(x_ref, st_ref, cm_ref, s0_ref, cw_ref, alog_ref, dtb_ref, onw_ref,
                       y_ref, s_ref, *, n_real):
    accs = []
    for g in range(SEQ_GROUP):
        ext = jnp.concatenate([st_ref[g], x_ref[g * SROWS:(g + 1) * SROWS, :]], axis=0)
        acc = jnp.zeros((SROWS, QKV_C), F32)
        for j in range(SHORT_CONV):
            acc = acc + cw_ref[j:j + 1, :] * ext[CONVC_OFF + j:CONVC_OFF + j + SROWS, :]
        accs.append(acc)
    rows = SEQ_GROUP * SROWS
    valid = lax.broadcasted_iota(jnp.int32, (rows, 1), 0) % SROWS < n_real
    states = [sp for g in range(SEQ_GROUP)
              for sp in _pack_states([s0_ref[g, h] for h in range(H_C)])]
    y, s_new = _gdn_block(_silu(jnp.concatenate(accs, axis=0)), cm_ref[...], alog_ref[...],
                          dtb_ref[...], onw_ref[...], states, SROWS, SEQ_GROUP, False, valid)
    y_ref[...] = y
    for g in range(SEQ_GROUP):
        for h in range(H_C):
            s_ref[g, h] = _unpack_state(s_new[g * N_PAIRS:(g + 1) * N_PAIRS], h)


def _gdn_sample(x, st, cm, s0, cw, alog, dtb, onw, n_real):
    nb = s0.shape[0]
    rows = SEQ_GROUP * SROWS
    full = lambda shp: pl.BlockSpec(shp, lambda g: (0,) * len(shp))
    tok = lambda wd: pl.BlockSpec((rows, wd), lambda g: (g, 0))
    state = pl.BlockSpec((SEQ_GROUP, H_C, DK_C, DV_C), lambda g: (g, 0, 0, 0))
    return pl.pallas_call(
        functools.partial(_gdn_sample_kernel, n_real=n_real),
        grid=(nb // SEQ_GROUP,),
        in_specs=[tok(QKV_C), pl.BlockSpec((SEQ_GROUP, CONVC_PRE, QKV_C), lambda g: (g, 0, 0)),
                  tok(W_CM), state, full((SHORT_CONV, QKV_C)), full((1, LANES)), full((1, LANES)),
                  full((1, DV_C))],
        out_specs=[tok(W_C), state],
        out_shape=[jax.ShapeDtypeStruct((nb * SROWS, W_C), F32),
                   jax.ShapeDtypeStruct((nb, H_C, DK_C, DV_C), F32)],
        compiler_params=_cparams(("parallel",)),
        name="gdn_sample",
    )(x, st, cm, s0, cw, alog, dtb, onw)


def _prepare_w_in(w_in):
    depth, d, _ = w_in.shape
    beta = w_in[:, :, C_CG:C_CG + H_C]
    alpha = w_in[:, :, C_CG + H_C:C_CG + 2 * H_C]
    gate = w_in[:, :, C_CG + 2 * H_C:]
    zpad = jnp.zeros((depth, d, LANES - H_C), w_in.dtype)
    wtail = jnp.concatenate([gate, beta, zpad, alpha, zpad], axis=-1).astype(BF16)
    wt = jnp.concatenate([w_in[:, :, C_Q:C_K], w_in[:, :, C_V:C_BG]], axis=-1)
    return w_in.astype(BF16), wtail, jnp.swapaxes(wt, 1, 2).astype(BF16)


def _pad_lanes(x):
    return jnp.pad(x, ((0, 0), (0, LANES - x.shape[-1])))


def kernel(x_prompt, x_sample, cache_k, cache_v, state_conv_a, state_conv_c, state_delta, page_table,
           norm_w, w_in, conv_a_w, conv_a_b, ln_a_w, ln_a_b, pw_a_w, pw_a_b,
           lambda_q1, lambda_k1, lambda_q2, lambda_k2, subln_w, conv_c_w, a_log, dt_bias,
           onorm_c_w, w_out, final_norm_w):
    bsz, seq, d = x_prompt.shape
    nb, ns, _ = x_sample.shape
    depth = w_in.shape[0]
    n_pool = cache_k.shape[1]
    n_pages = page_table.shape[1]
    t_attn = ATTN_TILE if seq % ATTN_TILE == 0 else seq
    tl_conv = 512 if seq % 512 == 0 else seq
    tm_p = 256
    tm_o = 1024 if (bsz * seq) % 1024 == 0 else tm_p
    nch = GDN_CHUNKS if seq % (GDN_CHUNKS * CHUNK) == 0 else 1
    npg = PAGES_PER_STEP if n_pages % PAGES_PER_STEP == 0 else 1
    assert SHORT_CONV - 1 <= ns <= SROWS, "sample tokens must fit one 8-row tile"
    assert nb % SEQ_GROUP == 0 and seq % CHUNK == 0 and seq % t_attn == 0 and (bsz * seq) % tm_p == 0
    assert tl_conv % CONV_RC == 0 and tl_conv >= CONV_PRE

    wp, wtail, wt = _prepare_w_in(w_in)
    wo = w_out.astype(BF16)
    pw = pw_a_w.astype(BF16)
    ck = cache_k.reshape(depth, n_pool, PAGE * H_B, DV_B)
    cv = cache_v.reshape(depth, n_pool, PAGE * H_B, DV_B)
    st_a = jnp.pad(state_conv_a, ((0, 0), (0, 0), (CONV_PRE - (CONV_A - 1), 0), (0, 0)))
    st_c = jnp.pad(state_conv_c, ((0, 0), (0, 0), (CONVC_PRE - (SHORT_CONV - 1), 0), (0, 0)))
    alog = _pad_lanes(a_log)
    dtb = _pad_lanes(dt_bias)
    fw = final_norm_w.reshape(1, d)

    hp = x_prompt.reshape(bsz * seq, d)
    hs = jnp.pad(x_sample, ((0, 0), (0, SROWS - ns), (0, 0))).reshape(nb * SROWS, d)
    outs_p = [[] for _ in range(5)]
    outs_s = [[] for _ in range(5)]
    kv_all = tuple(jnp.zeros((depth, bsz * seq * H_B, DV_B), F32) for _ in range(2))
    for l in range(depth):
        lam_init = 0.8 - 0.6 * math.exp(-0.3 * l)
        final = l == depth - 1
        row = lambda a: a[l].reshape(1, -1)
        lams = (row(lambda_q1), row(lambda_k1), row(lambda_q2), row(lambda_k2))

        a3, qt, k_all, k16, v_all, vt, bg, qkvc, cm = _proj(hp, row(norm_w), wp, wtail, wt, l, tm_p,
                                                            kv_all)
        kv_all = (k_all, v_all)
        ya, tail = _conva_prompt(a3, conv_a_w[l], row(conv_a_b), row(ln_a_w), row(ln_a_b), pw[l],
                                 row(pw_a_b), bsz, seq, tl_conv)
        yb = _attn_prompt(qt, k16, vt, bg, row(subln_w), *lams, lam_init, bsz, seq, t_attn)
        yc, s_new = _gdn_prompt(qkvc, cm, conv_c_w[l], alog[l:l + 1], dtb[l:l + 1], row(onorm_c_w),
                                bsz, seq, nch)
        hp = _outproj(ya, yb, yc, hp, wo[l], fw, tm_o, final)
        outs_p[2].append(tail[:, CONV_PRE - (CONV_A - 1):])
        outs_p[3].append(qkvc.reshape(bsz, seq, QKV_C)[:, seq - (SHORT_CONV - 1):])
        outs_p[4].append(s_new)

        a3, q, k32, v32, bg, qkvc, cm = _proj(hs, row(norm_w), wp, wtail, wt, l, nb * SROWS)
        ya, u = _conva_sample(a3.reshape(nb, SROWS, 3 * W_A), st_a[l], conv_a_w[l], row(conv_a_b),
                              row(ln_a_w), row(ln_a_b), pw[l], row(pw_a_b))
        yb = _attn_sample(page_table, q, ck, cv, k32, v32, bg, row(subln_w), *lams, lam_init, l, npg)
        yc, s_new = _gdn_sample(qkvc, st_c[l], cm, state_delta[l], conv_c_w[l], alog[l:l + 1],
                                dtb[l:l + 1], row(onorm_c_w), ns)
        hs = _outproj(ya.reshape(nb * SROWS, W_A), yb, yc, hs, wo[l], fw, nb * SROWS, final)
        outs_s[0].append(k32.reshape(nb, SROWS, H_B, 2 * DK_B)[:, :ns])
        outs_s[1].append(v32.reshape(nb, SROWS, H_B, DV_B)[:, :ns])
        outs_s[2].append(jnp.concatenate([state_conv_a[l], u[:, :ns]], axis=1)[:, -(CONV_A - 1):])
        outs_s[3].append(jnp.concatenate([state_conv_c[l], qkvc.reshape(nb, SROWS, QKV_C)[:, :ns]],
                                         axis=1)[:, -(SHORT_CONV - 1):])
        outs_s[4].append(s_new)

    y_prompt = hp.reshape(bsz, seq, d)
    y_sample = hs.reshape(nb, SROWS, d)[:, :ns]
    stack = lambda xs: jnp.stack(xs)
    return (y_prompt, y_sample,
            kv_all[0].reshape(depth, bsz, seq, H_B, 2 * DK_B), kv_all[1].reshape(depth, bsz, seq, H_B, DV_B),
            stack(outs_p[2]), stack(outs_p[3]),
            stack(outs_p[4]).astype(state_delta.dtype),
            stack(outs_s[0]), stack(outs_s[1]), stack(outs_s[2]), stack(outs_s[3]),
            stack(outs_s[4]).astype(state_delta.dtype))
```

```python
import functools
import math

import jax
import jax.numpy as jnp
from jax import lax
from jax.experimental import pallas as pl
from jax.experimental.pallas import tpu as pltpu

F32 = jnp.float32
BF16 = jnp.bfloat16

EPS = 1e-6
LOG2E = math.log2(math.e)
W_A = 256
CONV_A = 31
H_B = 4
DK_B = 64
DV_B = 128
W_B = H_B * DV_B
H_C = 4
DK_C = 64
DV_C = 64
W_C = H_C * DV_C
QKV_C = H_C * (2 * DK_C + DV_C)
SHORT_CONV = 4
CHUNK = 64
PAGE = 128
D_MIX = W_A + W_B + W_C

LANES = 128
SUBLANES = 8
SROWS = SUBLANES
VMEM_LIMIT = 48 * 1024 * 1024

C_A3 = 0
C_Q = C_A3 + 3 * W_A
C_K = C_Q + W_B
C_V = C_K + W_B
C_BG = C_V + W_B
C_QKVC = C_BG + W_B
C_CG = C_QKVC + QKV_C
W_CM = W_C + 2 * LANES


def _cparams(sem):
    return pltpu.CompilerParams(dimension_semantics=sem, vmem_limit_bytes=VMEM_LIMIT)


def _sigmoid(x):
    return 1.0 / (1.0 + jnp.exp(-x))


def _silu(x):
    return x * _sigmoid(x)


def _softplus(x):
    return jnp.maximum(x, 0.0) + jnp.log1p(jnp.exp(-jnp.abs(x)))


def _dot(a, b):
    return jnp.dot(a, b, preferred_element_type=F32)


def _dot_nt(a, b):
    return lax.dot_general(a, b, (((1,), (1,)), ((), ())), preferred_element_type=F32)


def _dot_tn(a, b):
    return lax.dot_general(a, b, (((0,), (0,)), ((), ())), preferred_element_type=F32)


def _each(f, *lists):
    return [f(*xs) for xs in zip(*lists)]


def _proj_kernel(x_ref, nw_ref, w_ref, wtail_ref, wt_ref, *refs, prompt):
    out_refs = refs[2:] if prompt else refs
    x = x_ref[...]
    ms = jnp.mean(x * x, axis=-1, keepdims=True)
    hn = (x * lax.rsqrt(ms + EPS) * nw_ref[...]).astype(BF16)

    def mm(lo, hi):
        return _dot(hn, w_ref[:, lo:hi])

    if prompt:
        a3_ref, qt_ref, k32_ref, k16_ref, v32_ref, vt_ref, bg_ref, qkvc_ref, cm_ref = out_refs
        qt_ref[...] = (_dot_nt(wt_ref[0:W_B, :], hn) * (LOG2E * DK_B ** -0.5)).astype(BF16)
        vt = _dot_nt(wt_ref[W_B:2 * W_B, :], hn)
        vt_ref[...] = vt.astype(BF16)
        v = vt.T
        k = mm(C_K, C_V)
        k16_ref[...] = k.astype(BF16)
    else:
        a3_ref, q_ref, k32_ref, v32_ref, bg_ref, qkvc_ref, cm_ref = out_refs
        q_ref[...] = mm(C_Q, C_K) * (DK_B ** -0.5)
        k = mm(C_K, C_V)
        v = mm(C_V, C_BG)
    a3_ref[...] = mm(C_A3, C_Q)
    tm = x.shape[0]
    for h in range(H_B):
        k32_ref[pl.ds(h, tm, stride=H_B), :] = k[:, h * DV_B:(h + 1) * DV_B]
        v32_ref[pl.ds(h, tm, stride=H_B), :] = v[:, h * DV_B:(h + 1) * DV_B]
    bg_ref[...] = mm(C_BG, C_QKVC)
    qkvc_ref[...] = mm(C_QKVC, C_CG)
    cm_ref[...] = _dot(hn, wtail_ref[...])


def _proj(x, nw, w, wtail, wt, layer, tm, kv_all=None):
    m, d = x.shape
    n_in = w.shape[-1]
    prompt = kv_all is not None
    per_layer = lambda r, c: pl.BlockSpec((None, r, c), lambda i: (layer, 0, 0))
    tok = lambda wd, dt: (pl.BlockSpec((tm, wd), lambda i: (i, 0)), jax.ShapeDtypeStruct((m, wd), dt))
    tr = lambda: (pl.BlockSpec((W_B, tm), lambda i: (0, i)), jax.ShapeDtypeStruct((W_B, m), BF16))
    hd = lambda: (pl.BlockSpec((tm * H_B, DV_B), lambda i: (i, 0)),
                  jax.ShapeDtypeStruct((m * H_B, DV_B), F32))
    hd_all = lambda a: (pl.BlockSpec((None, tm * H_B, DV_B), lambda i: (layer, i, 0)),
                        jax.ShapeDtypeStruct(a.shape, a.dtype))
    in_specs = [pl.BlockSpec((tm, d), lambda i: (i, 0)),
                pl.BlockSpec((1, d), lambda i: (0, 0)),
                per_layer(d, n_in), per_layer(d, W_CM), per_layer(2 * W_B, d)]
    args = [x, nw, w, wtail, wt]
    aliases = {}
    if prompt:
        outs = [tok(3 * W_A, F32), tr(), hd_all(kv_all[0]), tok(W_B, BF16), hd_all(kv_all[1]), tr(),
                tok(W_B, F32), tok(QKV_C, F32), tok(W_CM, F32)]
        in_specs += [pl.BlockSpec(memory_space=pl.ANY)] * 2
        aliases = {len(args): 2, len(args) + 1: 4}
        args += list(kv_all)
    else:
        outs = [tok(3 * W_A, F32), tok(W_B, F32), hd(), hd(),
                tok(W_B, F32), tok(QKV_C, F32), tok(W_CM, F32)]
    return pl.pallas_call(
        functools.partial(_proj_kernel, prompt=prompt),
        grid=(m // tm,),
        in_specs=in_specs,
        out_specs=[o[0] for o in outs],
        out_shape=[o[1] for o in outs],
        input_output_aliases=aliases,
        compiler_params=_cparams(("parallel",)),
        name="proj",
    )(*args)


def _outproj_kernel(ya_ref, yb_ref, yc_ref, h_ref, w_ref, fw_ref, o_ref, *, final):
    ycat = jnp.concatenate([ya_ref[...].astype(BF16), yb_ref[...].astype(BF16),
                            yc_ref[...].astype(BF16)], axis=-1)
    hnew = h_ref[...] + _dot(ycat, w_ref[...])
    if final:
        ms = jnp.mean(hnew * hnew, axis=-1, keepdims=True)
        hnew = hnew * lax.rsqrt(ms + EPS) * fw_ref[...]
    o_ref[...] = hnew


def _outproj(ya, yb, yc, h, w, fw, tm, final):
    m, d = h.shape
    row = lambda wd: pl.BlockSpec((tm, wd), lambda i: (i, 0))
    return pl.pallas_call(
        functools.partial(_outproj_kernel, final=final),
        grid=(m // tm,),
        in_specs=[row(W_A), row(W_B), row(W_C), row(d),
                  pl.BlockSpec((D_MIX, d), lambda i: (0, 0)),
                  pl.BlockSpec((1, d), lambda i: (0, 0))],
        out_specs=row(d),
        out_shape=jax.ShapeDtypeStruct((m, d), F32),
        compiler_params=_cparams(("parallel",)),
        name="outproj",
    )(ya, yb, yc, h, w, fw)


def _conva_tail(conv, a_gate, cb, lnw, lnb, pw, pb):
    y = conv + cb
    mu = jnp.mean(y, axis=-1, keepdims=True)
    yc = y - mu
    y = yc * lax.rsqrt(jnp.mean(yc * yc, axis=-1, keepdims=True) + EPS) * lnw + lnb
    y = _silu(y)
    y = _dot(y.astype(BF16), pw) + pb
    return y * _silu(a_gate)


CONV_PRE = 32
CONV_OFF = CONV_PRE - (CONV_A - 1)
CONV_RC = 64


def _conva_prompt_kernel(a3_ref, cw_ref, cb_ref, lnw_ref, lnb_ref, pw_ref, pb_ref,
                         ya_ref, tail_ref, ext_ref, *, tl):
    i = pl.program_id(1)

    @pl.when(i == 0)
    def _():
        ext_ref[0, 0:CONV_PRE, :] = jnp.zeros((CONV_PRE, W_A), F32)

    u = a3_ref[:, 0:W_A] * _sigmoid(a3_ref[:, W_A:2 * W_A])
    ext_ref[0, CONV_PRE:CONV_PRE + tl, :] = u
    n_shift = CONV_PRE + tl - SUBLANES
    for s in range(1, SUBLANES):
        ext_ref[s, 0:n_shift, :] = ext_ref[0, s:s + n_shift, :]
    for r in range(tl // CONV_RC):
        acc = jnp.zeros((CONV_RC, W_A), F32)
        for j in range(CONV_A):
            off = r * CONV_RC + CONV_OFF + j
            s = off % SUBLANES
            acc = acc + cw_ref[j:j + 1, :] * ext_ref[s, off - s:off - s + CONV_RC, :]
        gate = a3_ref[r * CONV_RC:(r + 1) * CONV_RC, 2 * W_A:3 * W_A]
        y = _conva_tail(acc, gate, cb_ref[...], lnw_ref[...], lnb_ref[...], pw_ref[...], pb_ref[...])
        ya_ref[r * CONV_RC:(r + 1) * CONV_RC, :] = y.astype(BF16)
    last = ext_ref[0, tl:tl + CONV_PRE, :]
    tail_ref[...] = last
    ext_ref[0, 0:CONV_PRE, :] = last


def _conva_prompt(a3, cw, cb, lnw, lnb, pw, pb, bsz, seq, tl):
    nl = seq // tl
    full = lambda shp: pl.BlockSpec(shp, lambda b, i: (0,) * len(shp))
    return pl.pallas_call(
        functools.partial(_conva_prompt_kernel, tl=tl),
        grid=(bsz, nl),
        in_specs=[pl.BlockSpec((tl, 3 * W_A), lambda b, i: (b * nl + i, 0)),
                  full((CONV_A, W_A)), full((1, W_A)), full((1, W_A)), full((1, W_A)),
                  full((W_A, W_A)), full((1, W_A))],
        out_specs=[pl.BlockSpec((tl, W_A), lambda b, i: (b * nl + i, 0)),
                   pl.BlockSpec((None, CONV_PRE, W_A), lambda b, i: (b, 0, 0))],
        out_shape=[jax.ShapeDtypeStruct((bsz * seq, W_A), BF16),
                   jax.ShapeDtypeStruct((bsz, CONV_PRE, W_A), F32)],
        scratch_shapes=[pltpu.VMEM((SUBLANES, CONV_PRE + tl, W_A), F32)],
        compiler_params=_cparams(("parallel", "arbitrary")),
        name="conva_prompt",
    )(a3, cw, cb, lnw, lnb, pw, pb)


SEQ_GROUP = 8


def _conva_sample_kernel(a3_ref, st_ref, cw_ref, cb_ref, lnw_ref, lnb_ref, pw_ref, pb_ref,
                         ya_ref, u_ref, ext_ref):
    a3 = a3_ref[...]
    u = a3[:, :, 0:W_A] * _sigmoid(a3[:, :, W_A:2 * W_A])
    u_ref[...] = u
    ext_ref[:, 0:CONV_PRE, :] = st_ref[...]
    ext_ref[:, CONV_PRE:CONV_PRE + SROWS, :] = u
    acc = jnp.zeros((SEQ_GROUP, SROWS, W_A), F32)
    for j in range(CONV_A):
        acc = acc + cw_ref[j:j + 1, :] * ext_ref[:, CONV_OFF + j:CONV_OFF + j + SROWS, :]
    rows = SEQ_GROUP * SROWS
    y = _conva_tail(acc.reshape(rows, W_A), a3[:, :, 2 * W_A:3 * W_A].reshape(rows, W_A),
                    cb_ref[...], lnw_ref[...], lnb_ref[...], pw_ref[...], pb_ref[...])
    ya_ref[...] = y.reshape(SEQ_GROUP, SROWS, W_A)


def _conva_sample(a3, st, cw, cb, lnw, lnb, pw, pb):
    nb = a3.shape[0]
    full = lambda shp: pl.BlockSpec(shp, lambda g: (0,) * len(shp))
    seqs = lambda r, wd: pl.BlockSpec((SEQ_GROUP, r, wd), lambda g: (g, 0, 0))
    return pl.pallas_call(
        _conva_sample_kernel,
        grid=(nb // SEQ_GROUP,),
        in_specs=[seqs(SROWS, 3 * W_A), seqs(CONV_PRE, W_A),
                  full((CONV_A, W_A)), full((1, W_A)), full((1, W_A)), full((1, W_A)),
                  full((W_A, W_A)), full((1, W_A))],
        out_specs=[seqs(SROWS, W_A), seqs(SROWS, W_A)],
        out_shape=[jax.ShapeDtypeStruct((nb, SROWS, W_A), F32),
                   jax.ShapeDtypeStruct((nb, SROWS, W_A), F32)],
        scratch_shapes=[pltpu.VMEM((SEQ_GROUP, CONV_PRE + SROWS, W_A), F32)],
        compiler_params=_cparams(("parallel",)),
        name="conva_sample",
    )(a3, st, cw, cb, lnw, lnb, pw, pb)


def _lambda_full(lq1, lk1, lq2, lk2, lam_init):
    s1 = jnp.sum(lq1 * lk1, axis=-1, keepdims=True)
    s2 = jnp.sum(lq2 * lk2, axis=-1, keepdims=True)
    return jnp.exp(s1) - jnp.exp(s2) + lam_init


def _subln_gate(o, sub, bg, lam_init):
    y = o * lax.rsqrt(jnp.mean(o * o, axis=-1, keepdims=True) + EPS) * sub
    return (y * (1.0 - lam_init)) * _silu(bg)


ATTN_TILE = 2048
ATTN_STRIP = 256


def _attn_prompt_kernel(qt_ref, k_ref, vt_ref, bg_ref, sub_ref, lq1_ref, lk1_ref, lq2_ref, lk2_ref,
                        o_ref, m1_ref, l1_ref, a1_ref, m2_ref, l2_ref, a2_ref, *, lam_init, t):
    qi = pl.program_id(2)
    qt = qt_ref[...]
    feat = lax.broadcasted_iota(jnp.int32, qt.shape, 0)
    zero = jnp.zeros_like(qt)
    qs = (jnp.where(feat < DK_B, qt, zero), jnp.where(feat >= DK_B, qt, zero))
    stats = ((m1_ref, l1_ref, a1_ref), (m2_ref, l2_ref, a2_ref))
    for m_ref, l_ref, a_ref in stats:
        m_ref[...] = jnp.full(m_ref.shape, -jnp.inf, F32)
        l_ref[...] = jnp.zeros(l_ref.shape, F32)
        a_ref[...] = jnp.zeros(a_ref.shape, F32)

    sw = min(ATTN_STRIP, t)
    units = [(mp, c) for c in range(t // sw) for mp in (0, 1)]

    def tile(j, causal):
        start = pl.multiple_of(j * t, t)
        k = k_ref[pl.ds(start, t), :]
        vt = vt_ref[:, pl.ds(start, t)]
        nkeys = lambda c: (c + 1) * sw if causal else t

        def scores(u):
            mp, c = u
            return _dot(k[:nkeys(c)], qs[mp][:, c * sw:(c + 1) * sw])

        def softmax(u, st):
            mp, c = u
            m_ref, l_ref, _ = stats[mp]
            cols = slice(c * sw, (c + 1) * sw)
            if causal:
                kr = lax.broadcasted_iota(jnp.int32, st.shape, 0)
                qc = lax.broadcasted_iota(jnp.int32, st.shape, 1) + c * sw
                st = jnp.where(kr <= qc, st, -jnp.inf)
            m_prev = m_ref[:, cols]
            m_new = jnp.maximum(m_prev, jnp.max(st, axis=0, keepdims=True))
            alpha = jnp.exp2(m_prev - m_new)
            pt = jnp.exp2(st - m_new)
            l_ref[:, cols] = alpha * l_ref[:, cols] + jnp.sum(pt, axis=0, keepdims=True)
            m_ref[:, cols] = m_new
            return pt.astype(BF16), alpha

        def weighted_values(u, pt, alpha):
            mp, c = u
            a_ref = stats[mp][2]
            cols = slice(c * sw, (c + 1) * sw)
            a_ref[:, cols] = alpha * a_ref[:, cols] + _dot(vt[:, :nkeys(c)], pt)

        n = len(units)
        sts, pts = {}, {}
        for i in range(n + 2):
            if i < n:
                sts[i] = scores(units[i])
            if 0 <= i - 1 < n:
                pts[i - 1] = softmax(units[i - 1], sts.pop(i - 1))
            if 0 <= i - 2 < n:
                weighted_values(units[i - 2], *pts.pop(i - 2))

    def body(j, carry):
        tile(j, False)
        return carry

    lax.fori_loop(0, qi, body, 0)
    tile(qi, True)
    lam = _lambda_full(lq1_ref[...], lk1_ref[...], lq2_ref[...], lk2_ref[...], lam_init)
    ot = a1_ref[...] / l1_ref[...] - lam * (a2_ref[...] / l2_ref[...])
    o_ref[...] = _subln_gate(ot.T, sub_ref[...], bg_ref[...], lam_init).astype(BF16)


def _attn_prompt(qt, k, vt, bg, sub, lq1, lk1, lq2, lk2, lam_init, bsz, seq, t):
    n = seq // t
    full = lambda shp: pl.BlockSpec(shp, lambda b, h, qi: (0,) * len(shp))
    return pl.pallas_call(
        functools.partial(_attn_prompt_kernel, lam_init=lam_init, t=t),
        grid=(bsz, H_B, n),
        in_specs=[pl.BlockSpec((DV_B, t), lambda b, h, qi: (h, b * n + qi)),
                  pl.BlockSpec((seq, DV_B), lambda b, h, qi: (b, h)),
                  pl.BlockSpec((DV_B, seq), lambda b, h, qi: (h, b)),
                  pl.BlockSpec((t, DV_B), lambda b, h, qi: (b * n + qi, h)),
                  full((1, DV_B)), full((1, DK_B)), full((1, DK_B)), full((1, DK_B)),
                  full((1, DK_B))],
        out_specs=pl.BlockSpec((t, DV_B), lambda b, h, qi: (b * n + qi, h)),
        out_shape=jax.ShapeDtypeStruct((bsz * seq, W_B), BF16),
        scratch_shapes=[pltpu.VMEM((1, t), F32), pltpu.VMEM((1, t), F32), pltpu.VMEM((DV_B, t), F32),
                        pltpu.VMEM((1, t), F32), pltpu.VMEM((1, t), F32), pltpu.VMEM((DV_B, t), F32)],
        compiler_params=_cparams(("parallel", "parallel", "arbitrary")),
        name="attn_prompt",
    )(qt, k, vt, bg, sub, lq1, lk1, lq2, lk2)


PAGES_PER_STEP = 32
N_QROWS = 2 * SROWS


def _attn_sample_kernel(pt_ref, q_ref, *refs, lam_init, n_steps, npg):
    kp_refs = refs[:npg]
    vp_refs = refs[npg:2 * npg]
    (kn_ref, vn_ref, bg_ref, sub_ref, lq1_ref, lk1_ref, lq2_ref, lk2_ref, o_ref,
     qb_ref, m_ref, l_ref, acc_ref) = refs[2 * npg:]
    p = pl.program_id(1)
    heads = range(H_B)

    @pl.when(p == 0)
    def _():
        q = q_ref[...]
        lane = lax.broadcasted_iota(jnp.int32, (SROWS, DV_B), 1)
        for h in heads:
            qh = q[:, h * DV_B:(h + 1) * DV_B]
            qb_ref[h, 0:SROWS, :] = jnp.where(lane < DK_B, qh, 0.0)
            qb_ref[h, SROWS:N_QROWS, :] = jnp.where(lane >= DK_B, qh, 0.0)
        m_ref[...] = jnp.full(m_ref.shape, -jnp.inf, F32)
        l_ref[...] = jnp.zeros(l_ref.shape, F32)
        acc_ref[...] = jnp.zeros(acc_ref.shape, F32)

    def update_all(ss, vs):
        m_prev = [m_ref[h] for h in heads]
        m_new = [jnp.maximum(mp, jnp.max(s, axis=-1, keepdims=True)) for mp, s in zip(m_prev, ss)]
        alpha = [jnp.exp(mp - mn) for mp, mn in zip(m_prev, m_new)]
        pr = [jnp.exp(s - mn) for s, mn in zip(ss, m_new)]
        pv = [_dot(x.astype(BF16), v) for x, v in zip(pr, vs)]
        for h in heads:
            l_ref[h] = alpha[h] * l_ref[h] + jnp.sum(pr[h], axis=-1, keepdims=True)
            acc_ref[h] = alpha[h] * acc_ref[h] + pv[h]
            m_ref[h] = m_new[h]

    def head_rows(page_refs, h):
        return jnp.concatenate([r[pl.ds(h, PAGE, stride=H_B), :].astype(BF16) for r in page_refs], axis=0)

    ks = [head_rows(kp_refs, h) for h in heads]
    ss = [_dot_nt(qb_ref[h].astype(BF16), ks[h]) for h in heads]
    update_all(ss, [head_rows(vp_refs, h) for h in heads])

    @pl.when(p == n_steps - 1)
    def _():
        ss, vs = [], []
        for h in heads:
            kn = kn_ref[pl.ds(h, SROWS, stride=H_B), :].astype(BF16)
            s = _dot_nt(qb_ref[h].astype(BF16), kn)
            tq = lax.broadcasted_iota(jnp.int32, s.shape, 0) % SROWS
            tk = lax.broadcasted_iota(jnp.int32, s.shape, 1)
            ss.append(jnp.where(tk <= tq, s, -jnp.inf))
            vs.append(vn_ref[pl.ds(h, SROWS, stride=H_B), :].astype(BF16))
        update_all(ss, vs)
        lam = _lambda_full(lq1_ref[...], lk1_ref[...], lq2_ref[...], lk2_ref[...], lam_init)
        outs = []
        for h in heads:
            an = acc_ref[h] / l_ref[h]
            o = an[0:SROWS] - lam * an[SROWS:N_QROWS]
            outs.append(_subln_gate(o, sub_ref[...], bg_ref[:, h * DV_B:(h + 1) * DV_B], lam_init))
        o_ref[...] = jnp.concatenate(outs, axis=-1)


def _attn_sample(page_table, q, cache_k, cache_v, kn, vn, bg, sub, lq1, lk1, lq2, lk2, lam_init,
                 layer, npg):
    nb, n_pages = page_table.shape
    n_steps = n_pages // npg
    seq = lambda b, p, pt: (b, 0)

    def page(i):
        return lambda b, p, pt: (layer, pt[b, p * npg + i], 0, 0)

    full = lambda shp: pl.BlockSpec(shp, lambda b, p, pt: (0,) * len(shp))
    pages = [pl.BlockSpec((None, None, PAGE * H_B, DV_B), page(i)) for i in range(npg)]
    tok = pl.BlockSpec((SROWS, W_B), seq)
    new = pl.BlockSpec((SROWS * H_B, DV_B), seq)
    grid_spec = pltpu.PrefetchScalarGridSpec(
        num_scalar_prefetch=1,
        grid=(nb, n_steps),
        in_specs=[tok] + pages + pages + [new, new, tok, full((1, DV_B)), full((1, DK_B)),
                                          full((1, DK_B)), full((1, DK_B)), full((1, DK_B))],
        out_specs=tok,
        scratch_shapes=[pltpu.VMEM((H_B, N_QROWS, DV_B), F32), pltpu.VMEM((H_B, N_QROWS, 1), F32),
                        pltpu.VMEM((H_B, N_QROWS, 1), F32), pltpu.VMEM((H_B, N_QROWS, DV_B), F32)],
    )
    return pl.pallas_call(
        functools.partial(_attn_sample_kernel, lam_init=lam_init, n_steps=n_steps, npg=npg),
        grid_spec=grid_spec,
        out_shape=jax.ShapeDtypeStruct((nb * SROWS, W_B), F32),
        compiler_params=_cparams(("parallel", "arbitrary")),
        name="attn_sample",
    )(page_table, q, *([cache_k] * npg), *([cache_v] * npg), kn, vn, bg, sub, lq1, lk1, lq2, lk2)


def _split(x):
    hi = x.astype(BF16)
    return hi, (x - hi.astype(F32)).astype(BF16)


def _dot3(a, b):
    ah, al = a
    bh, bl = b
    return _dot(ah, bh) + (_dot(ah, bl) + _dot(al, bh))


def _halves(x, cut):
    lane = lax.broadcasted_iota(jnp.int32, x.shape, x.ndim - 1)
    z = jnp.zeros_like(x)
    return jnp.where(lane < cut, x, z), jnp.where(lane >= cut, x, z)


def _blockdiag(x, cut):
    return jnp.concatenate(_halves(x, cut), axis=0)


def _blockdiag_split(xs, cut):
    return _blockdiag(xs[0], cut), _blockdiag(xs[1], cut)


def _pair_lanes(x0, x1, width, cut):
    lane = lax.broadcasted_iota(jnp.int32, (x0.shape[0], width), 1)
    return jnp.where(lane < cut, x0, x1)


def _tri_inverse_many(a_list, c):
    row = lax.broadcasted_iota(jnp.int32, (c, 2 * c), 0)
    lane = lax.broadcasted_iota(jnp.int32, (c, 2 * c), 1)
    col = jnp.where(lane >= c, lane - c, lane)
    eye = (row == col).astype(F32)
    nb = min(c, 16)
    bd = lambda xs: _blockdiag_split(xs, c)
    ad = [jnp.where(row // nb == col // nb, a, 0.0) for a in a_list]
    t = [eye - x for x in ad]
    pw = ad
    size = 2
    while size < nb:
        ps = _each(_split, pw)
        pw = _each(_dot3, ps, _each(bd, ps))
        tp = _each(_dot3, _each(_split, t), _each(bd, _each(_split, pw)))
        t = _each(lambda x, y: x + y, t, tp)
        size *= 2
    size = nb * 2
    while size <= c:
        off = jnp.logical_and(row // size == col // size, row // (size // 2) != col // (size // 2))
        ts = _each(_split, t)
        tb = _each(bd, ts)
        inner = _each(_dot3, [_split(jnp.where(off, a, 0.0)) for a in a_list], tb)
        outer = _each(_dot3, ts, _each(bd, _each(_split, inner)))
        t = _each(lambda x, y: x - y, t, outer)
        size *= 2
    return t


def _delta_prep_many(items, c):
    dh = DK_C
    row = lax.broadcasted_iota(jnp.int32, (c, 2 * c), 0)
    lane = lax.broadcasted_iota(jnp.int32, (c, 2 * c), 1)
    col = jnp.where(lane >= c, lane - c, lane)
    decay = [jnp.exp(jnp.where(row >= col, it[3] - it[5], -jnp.inf)) for it in items]
    kb = [it[1].astype(BF16) for it in items]
    kbd = [_blockdiag(x, dh) for x in kb]
    kk = _each(_dot_nt, kb, kbd)
    a = [jnp.where(row > col, it[6] * x * d, 0.0) for it, x, d in zip(items, kk, decay)]
    t = _tri_inverse_many(a, c)
    eg = [jnp.exp(it[4]) for it in items]
    rhs = []
    for it, e in zip(items, eg):
        vb = _split(it[2] * it[7])
        kbe = _split(it[1] * (it[7] * e))
        rhs.append(tuple(jnp.concatenate([_blockdiag(x, dh), _blockdiag(y, dh)], axis=-1)
                         for x, y in zip(vb, kbe)))
    uw = _each(_dot3, _each(_split, t), rhs)
    qk = _each(_dot_nt, [it[0].astype(BF16) for it in items], kbd)
    out = []
    for it, x, e, s, d in zip(items, uw, eg, qk, decay):
        q, k, gf = it[0], it[1], it[4]
        glast = gf[c - 1:c, :]
        wq = jnp.concatenate([x[:, 2 * dh:], q * e], axis=0).astype(BF16)
        kt = (k * jnp.exp(glast - gf)).astype(BF16)
        out.append((x[:, :2 * dh], wq, (s * d).astype(BF16), kt, jnp.exp(glast)))
    return out


def _delta_scan(u, wq, qk, kt, gl, s, c):
    r = _dot(wq, _blockdiag(s.astype(BF16), DV_C))
    vnb = (u - r[:c]).astype(BF16)
    o = r[c:] + _dot(qk, _blockdiag(vnb, DV_C))
    full = _dot_tn(kt, vnb)
    lane = lax.broadcasted_iota(jnp.int32, s.shape, 1)
    return o, s * gl + jnp.where(lane < DV_C, full[:DK_C], full[DK_C:])


def _pair_norm(x, cut):
    lo, hi = _halves(x * x, cut)
    lane = lax.broadcasted_iota(jnp.int32, x.shape, 1)
    return jnp.where(lane < cut, jnp.sum(lo, axis=-1, keepdims=True),
                     jnp.sum(hi, axis=-1, keepdims=True))


N_PAIRS = H_C // 2


def _gdn_block(x, cm, alog, dtb, onw, states, c, nseg, chained, valid):
    rows = nseg * c
    pw_ = 2 * DK_C
    beta = _sigmoid(cm[:, W_C:W_C + LANES])
    g = -jnp.exp(alog) * _softplus(cm[:, W_C + LANES:W_C + 2 * LANES] + dtb)
    if valid is not None:
        beta = jnp.where(valid, beta, 0.0)
        g = jnp.where(valid, g, 0.0)
        x = jnp.where(valid, x, 0.0)
    row = lax.broadcasted_iota(jnp.int32, (rows, rows), 0)
    col = lax.broadcasted_iota(jnp.int32, (rows, rows), 1)
    same = row // c == col // c
    lower = jnp.logical_and(same, row >= col).astype(F32).astype(BF16)
    upper = jnp.logical_and(same, row <= col).astype(F32).astype(BF16)
    g1 = g.astype(BF16)
    r1 = g - g1.astype(F32)
    g2 = r1.astype(BF16)
    g3 = (r1 - g2.astype(F32)).astype(BF16)
    gcs = _dot(lower, g1) + (_dot(lower, g2) + _dot(lower, g3))
    gcs_t = _dot_tn(g1, upper) + (_dot_tn(g2, upper) + _dot_tn(g3, upper))
    qs, ks = [], []
    for p in range(N_PAIRS):
        xq = x[:, p * pw_:(p + 1) * pw_]
        xk = x[:, W_C + p * pw_:W_C + (p + 1) * pw_]
        qs.append(xq * lax.rsqrt(_pair_norm(xq, DK_C) + EPS) * (DK_C ** -0.5))
        ks.append(xk * lax.rsqrt(_pair_norm(xk, DK_C) + EPS))
    keys = [(i, p) for i in range(nseg) for p in range(N_PAIRS)]
    seg = lambda a, i: a[i * c:(i + 1) * c]
    items = []
    for i, p in keys:
        h0, h1 = 2 * p, 2 * p + 1
        g0, g1 = seg(gcs, i)[:, h0:h0 + 1], seg(gcs, i)[:, h1:h1 + 1]
        b0, b1 = seg(beta, i)[:, h0:h0 + 1], seg(beta, i)[:, h1:h1 + 1]
        grow = jnp.concatenate([gcs_t[h0:h0 + 1, i * c:(i + 1) * c],
                                gcs_t[h1:h1 + 1, i * c:(i + 1) * c]], axis=-1)
        items.append((seg(qs[p], i), seg(ks[p], i),
                      seg(x, i)[:, 2 * W_C + p * pw_:2 * W_C + (p + 1) * pw_],
                      _pair_lanes(g0, g1, 2 * c, c), _pair_lanes(g0, g1, pw_, DK_C), grow,
                      _pair_lanes(b0, b1, 2 * c, c), _pair_lanes(b0, b1, pw_, DK_C)))
    preps = dict(zip(keys, _delta_prep_many(items, c)))
    onw2 = jnp.concatenate([onw, onw], axis=-1)
    s_cur = list(states)
    ys = []
    for i in range(nseg):
        outs = []
        for p in range(N_PAIRS):
            si = p if chained else i * N_PAIRS + p
            o, s_cur[si] = _delta_scan(*preps[i, p], s_cur[si], c)
            o = o * lax.rsqrt(_pair_norm(o, DV_C) * (1.0 / DV_C) + EPS) * onw2
            outs.append(o * _silu(seg(cm, i)[:, p * pw_:(p + 1) * pw_]))
        ys.append(jnp.concatenate(outs, axis=-1))
    return jnp.concatenate(ys, axis=0), s_cur


def _pack_states(per_head):
    return [jnp.concatenate([per_head[2 * p], per_head[2 * p + 1]], axis=-1) for p in range(N_PAIRS)]


def _unpack_state(packed, h):
    return packed[h // 2][:, (h % 2) * DV_C:(h % 2 + 1) * DV_C]


CONVC_PRE = SUBLANES
CONVC_OFF = CONVC_PRE - (SHORT_CONV - 1)
GDN_CHUNKS = 4


def _gdn_prompt_kernel(x_ref, cm_ref, cw_ref, alog_ref, dtb_ref, onw_ref, y_ref, s_ref,
                       ext_ref, st_ref, *, nch):
    i = pl.program_id(1)
    rows = nch * CHUNK

    @pl.when(i == 0)
    def _():
        ext_ref[0:CONVC_PRE, :] = jnp.zeros((CONVC_PRE, QKV_C), F32)
        st_ref[...] = jnp.zeros(st_ref.shape, F32)

    ext_ref[CONVC_PRE:CONVC_PRE + rows, :] = x_ref[...]
    acc = jnp.zeros((rows, QKV_C), F32)
    for j in range(SHORT_CONV):
        acc = acc + cw_ref[j:j + 1, :] * ext_ref[CONVC_OFF + j:CONVC_OFF + j + rows, :]
    ext_ref[0:CONVC_PRE, :] = ext_ref[rows:rows + CONVC_PRE, :]
    y, s_new = _gdn_block(_silu(acc), cm_ref[...], alog_ref[...], dtb_ref[...], onw_ref[...],
                          [st_ref[p] for p in range(N_PAIRS)], CHUNK, nch, True, None)
    y_ref[...] = y.astype(BF16)
    for p in range(N_PAIRS):
        st_ref[p] = s_new[p]

    @pl.when(i == pl.num_programs(1) - 1)
    def _():
        for h in range(H_C):
            s_ref[h] = _unpack_state(s_new, h)


def _gdn_prompt(x, cm, cw, alog, dtb, onw, bsz, seq, nch):
    rows = nch * CHUNK
    nc = seq // rows
    full = lambda shp: pl.BlockSpec(shp, lambda b, i: (0,) * len(shp))
    tok = lambda wd: pl.BlockSpec((rows, wd), lambda b, i: (b * nc + i, 0))
    return pl.pallas_call(
        functools.partial(_gdn_prompt_kernel, nch=nch),
        grid=(bsz, nc),
        in_specs=[tok(QKV_C), tok(W_CM), full((SHORT_CONV, QKV_C)), full((1, LANES)),
                  full((1, LANES)), full((1, DV_C))],
        out_specs=[tok(W_C), pl.BlockSpec((None, H_C, DK_C, DV_C), lambda b, i: (b, 0, 0, 0))],
        out_shape=[jax.ShapeDtypeStruct((bsz * seq, W_C), BF16),
                   jax.ShapeDtypeStruct((bsz, H_C, DK_C, DV_C), F32)],
        scratch_shapes=[pltpu.VMEM((CONVC_PRE + rows, QKV_C), F32),
                        pltpu.VMEM((N_PAIRS, DK_C, 2 * DV_C), F32)],
        compiler_params=_cparams(("parallel", "arbitrary")),
        name="gdn_prompt",
    )(x, cm, cw, alog, dtb, onw)


def _gdn_sample_kernel(x_ref, st_ref, cm_ref, s0_ref, cw_ref, alog_ref, dtb_ref, onw_ref,
                       y_ref, s_ref, *, n_real):
    accs = []
    for g in range(SEQ_GROUP):
        ext = jnp.concatenate([st_ref[g], x_ref[g * SROWS:(g + 1) * SROWS, :]], axis=0)
        acc = jnp.zeros((SROWS, QKV_C), F32)
        for j in range(SHORT_CONV):
            acc = acc + cw_ref[j:j + 1, :] * ext[CONVC_OFF + j:CONVC_OFF + j + SROWS, :]
        accs.append(acc)
    rows = SEQ_GROUP * SROWS
    valid = lax.broadcasted_iota(jnp.int32, (rows, 1), 0) % SROWS < n_real
    states = [sp for g in range(SEQ_GROUP)
              for sp in _pack_states([s0_ref[g, h] for h in range(H_C)])]
    y, s_new = _gdn_block(_silu(jnp.concatenate(accs, axis=0)), cm_ref[...], alog_ref[...],
                          dtb_ref[...], onw_ref[...], states, SROWS, SEQ_GROUP, False, valid)
    y_ref[...] = y
    for g in range(SEQ_GROUP):
        for h in range(H_C):
            s_ref[g, h] = _unpack_state(s_new[g * N_PAIRS:(g + 1) * N_PAIRS], h)


def _gdn_sample(x, st, cm, s0, cw, alog, dtb, onw, n_real):
    nb = s0.shape[0]
    rows = SEQ_GROUP * SROWS
    full = lambda shp: pl.BlockSpec(shp, lambda g: (0,) * len(shp))
    tok = lambda wd: pl.BlockSpec((rows, wd), lambda g: (g, 0))
    state = pl.BlockSpec((SEQ_GROUP, H_C, DK_C, DV_C), lambda g: (g, 0, 0, 0))
    return pl.pallas_call(
        functools.partial(_gdn_sample_kernel, n_real=n_real),
        grid=(nb // SEQ_GROUP,),
        in_specs=[tok(QKV_C), pl.BlockSpec((SEQ_GROUP, CONVC_PRE, QKV_C), lambda g: (g, 0, 0)),
                  tok(W_CM), state, full((SHORT_CONV, QKV_C)), full((1, LANES)), full((1, LANES)),
                  full((1, DV_C))],
        out_specs=[tok(W_C), state],
        out_shape=[jax.ShapeDtypeStruct((nb * SROWS, W_C), F32),
                   jax.ShapeDtypeStruct((nb, H_C, DK_C, DV_C), F32)],
        compiler_params=_cparams(("parallel",)),
        name="gdn_sample",
    )(x, st, cm, s0, cw, alog, dtb, onw)


def _prepare_w_in(w_in):
    depth, d, _ = w_in.shape
    beta = w_in[:, :, C_CG:C_CG + H_C]
    alpha = w_in[:, :, C_CG + H_C:C_CG + 2 * H_C]
    gate = w_in[:, :, C_CG + 2 * H_C:]
    zpad = jnp.zeros((depth, d, LANES - H_C), w_in.dtype)
    wtail = jnp.concatenate([gate, beta, zpad, alpha, zpad], axis=-1).astype(BF16)
    wt = jnp.concatenate([w_in[:, :, C_Q:C_K], w_in[:, :, C_V:C_BG]], axis=-1)
    return w_in.astype(BF16), wtail, jnp.swapaxes(wt, 1, 2).astype(BF16)


def _pad_lanes(x):
    return jnp.pad(x, ((0, 0), (0, LANES - x.shape[-1])))


def kernel(x_prompt, x_sample, cache_k, cache_v, state_conv_a, state_conv_c, state_delta, page_table,
           norm_w, w_in, conv_a_w, conv_a_b, ln_a_w, ln_a_b, pw_a_w, pw_a_b,
           lambda_q1, lambda_k1, lambda_q2, lambda_k2, subln_w, conv_c_w, a_log, dt_bias,
           onorm_c_w, w_out, final_norm_w):
    bsz, seq, d = x_prompt.shape
    nb, ns, _ = x_sample.shape
    depth = w_in.shape[0]
    n_pool = cache_k.shape[1]
    n_pages = page_table.shape[1]
    t_attn = ATTN_TILE if seq % ATTN_TILE == 0 else seq
    tl_conv = 512 if seq % 512 == 0 else seq
    tm_p = 256
    tm_o = 1024 if (bsz * seq) % 1024 == 0 else tm_p
    nch = GDN_CHUNKS if seq % (GDN_CHUNKS * CHUNK) == 0 else 1
    npg = PAGES_PER_STEP if n_pages % PAGES_PER_STEP == 0 else 1
    assert SHORT_CONV - 1 <= ns <= SROWS, "sample tokens must fit one 8-row tile"
    assert nb % SEQ_GROUP == 0 and seq % CHUNK == 0 and seq % t_attn == 0 and (bsz * seq) % tm_p == 0
    assert tl_conv % CONV_RC == 0 and tl_conv >= CONV_PRE

    wp, wtail, wt = _prepare_w_in(w_in)
    wo = w_out.astype(BF16)
    pw = pw_a_w.astype(BF16)
    ck = cache_k.reshape(depth, n_pool, PAGE * H_B, DV_B)
    cv = cache_v.reshape(depth, n_pool, PAGE * H_B, DV_B)
    st_a = jnp.pad(state_conv_a, ((0, 0), (0, 0), (CONV_PRE - (CONV_A - 1), 0), (0, 0)))
    st_c = jnp.pad(state_conv_c, ((0, 0), (0, 0), (CONVC_PRE - (SHORT_CONV - 1), 0), (0, 0)))
    alog = _pad_lanes(a_log)
    dtb = _pad_lanes(dt_bias)
    fw = final_norm_w.reshape(1, d)

    hp = x_prompt.reshape(bsz * seq, d)
    hs = jnp.pad(x_sample, ((0, 0), (0, SROWS - ns), (0, 0))).reshape(nb * SROWS, d)
    outs_p = [[] for _ in range(5)]
    outs_s = [[] for _ in range(5)]
    kv_all = tuple(jnp.zeros((depth, bsz * seq * H_B, DV_B), F32) for _ in range(2))
    for l in range(depth):
        lam_init = 0.8 - 0.6 * math.exp(-0.3 * l)
        final = l == depth - 1
        row = lambda a: a[l].reshape(1, -1)
        lams = (row(lambda_q1), row(lambda_k1), row(lambda_q2), row(lambda_k2))

        a3, qt, k_all, k16, v_all, vt, bg, qkvc, cm = _proj(hp, row(norm_w), wp, wtail, wt, l, tm_p,
                                                            kv_all)
        kv_all = (k_all, v_all)
        ya, tail = _conva_prompt(a3, conv_a_w[l], row(conv_a_b), row(ln_a_w), row(ln_a_b), pw[l],
                                 row(pw_a_b), bsz, seq, tl_conv)
        yb = _attn_prompt(qt, k16, vt, bg, row(subln_w), *lams, lam_init, bsz, seq, t_attn)
        yc, s_new = _gdn_prompt(qkvc, cm, conv_c_w[l], alog[l:l + 1], dtb[l:l + 1], row(onorm_c_w),
                                bsz, seq, nch)
        hp = _outproj(ya, yb, yc, hp, wo[l], fw, tm_o, final)
        outs_p[2].append(tail[:, CONV_PRE - (CONV_A - 1):])
        outs_p[3].append(qkvc.reshape(bsz, seq, QKV_C)[:, seq - (SHORT_CONV - 1):])
        outs_p[4].append(s_new)

        a3, q, k32, v32, bg, qkvc, cm = _proj(hs, row(norm_w), wp, wtail, wt, l, nb * SROWS)
        ya, u = _conva_sample(a3.reshape(nb, SROWS, 3 * W_A), st_a[l], conv_a_w[l], row(conv_a_b),
                              row(ln_a_w), row(ln_a_b), pw[l], row(pw_a_b))
        yb = _attn_sample(page_table, q, ck, cv, k32, v32, bg, row(subln_w), *lams, lam_init, l, npg)
        yc, s_new = _gdn_sample(qkvc, st_c[l], cm, state_delta[l], conv_c_w[l], alog[l:l + 1],
                                dtb[l:l + 1], row(onorm_c_w), ns)
        hs = _outproj(ya.reshape(nb * SROWS, W_A), yb, yc, hs, wo[l], fw, nb * SROWS, final)
        outs_s[0].append(k32.reshape(nb, SROWS, H_B, 2 * DK_B)[:, :ns])
        outs_s[1].append(v32.reshape(nb, SROWS, H_B, DV_B)[:, :ns])
        outs_s[2].append(jnp.concatenate([state_conv_a[l], u[:, :ns]], axis=1)[:, -(CONV_A - 1):])
        outs_s[3].append(jnp.concatenate([state_conv_c[l], qkvc.reshape(nb, SROWS, QKV_C)[:, :ns]],
                                         axis=1)[:, -(SHORT_CONV - 1):])
        outs_s[4].append(s_new)

    y_prompt = hp.reshape(bsz, seq, d)
    y_sample = hs.reshape(nb, SROWS, d)[:, :ns]
    stack = lambda xs: jnp.stack(xs)
    return (y_prompt, y_sample,
            kv_all[0].reshape(depth, bsz, seq, H_B, 2 * DK_B), kv_all[1].reshape(depth, bsz, seq, H_B, DV_B),
            stack(outs_p[2]), stack(outs_p[3]),
            stack(outs_p[4]).astype(state_delta.dtype),
            stack(outs_s[0]), stack(outs_s[1]), stack(outs_s[2]), stack(outs_s[3]),
            stack(outs_s[4]).astype(state_delta.dtype))
```

```python
import functools
import math

import jax
import jax.numpy as jnp
from jax import lax
from jax.experimental import pallas as pl
from jax.experimental.pallas import tpu as pltpu

F32 = jnp.float32
BF16 = jnp.bfloat16

EPS = 1e-6
LOG2E = math.log2(math.e)
W_A = 256
CONV_A = 31
H_B = 4
DK_B = 64
DV_B = 128
W_B = H_B * DV_B
H_C = 4
DK_C = 64
DV_C = 64
W_C = H_C * DV_C
QKV_C = H_C * (2 * DK_C + DV_C)
SHORT_CONV = 4
CHUNK = 64
PAGE = 128
D_MIX = W_A + W_B + W_C

LANES = 128
SUBLANES = 8
SROWS = SUBLANES
VMEM_LIMIT = 48 * 1024 * 1024

C_A3 = 0
C_Q = C_A3 + 3 * W_A
C_K = C_Q + W_B
C_V = C_K + W_B
C_BG = C_V + W_B
C_QKVC = C_BG + W_B
C_CG = C_QKVC + QKV_C
W_CM = W_C + 2 * LANES


def _cparams(sem):
    return pltpu.CompilerParams(dimension_semantics=sem, vmem_limit_bytes=VMEM_LIMIT)


def _sigmoid(x):
    return 1.0 / (1.0 + jnp.exp(-x))


def _silu(x):
    return x * _sigmoid(x)


def _softplus(x):
    return jnp.maximum(x, 0.0) + jnp.log1p(jnp.exp(-jnp.abs(x)))


def _dot(a, b):
    return jnp.dot(a, b, preferred_element_type=F32)


def _dot_nt(a, b):
    return lax.dot_general(a, b, (((1,), (1,)), ((), ())), preferred_element_type=F32)


def _dot_tn(a, b):
    return lax.dot_general(a, b, (((0,), (0,)), ((), ())), preferred_element_type=F32)


def _each(f, *lists):
    return [f(*xs) for xs in zip(*lists)]


def _proj_kernel(x_ref, nw_ref, w_ref, wtail_ref, wt_ref, *refs, prompt):
    out_refs = refs[2:] if prompt else refs
    x = x_ref[...]
    ms = jnp.mean(x * x, axis=-1, keepdims=True)
    hn = (x * lax.rsqrt(ms + EPS) * nw_ref[...]).astype(BF16)

    def mm(lo, hi):
        return _dot(hn, w_ref[:, lo:hi])

    if prompt:
        a3_ref, qt_ref, k32_ref, k16_ref, v32_ref, vt_ref, bg_ref, qkvc_ref, cm_ref = out_refs
        qt_ref[...] = (_dot_nt(wt_ref[0:W_B, :], hn) * (LOG2E * DK_B ** -0.5)).astype(BF16)
        vt = _dot_nt(wt_ref[W_B:2 * W_B, :], hn)
        vt_ref[...] = vt.astype(BF16)
        v = vt.T
        k = mm(C_K, C_V)
        k16_ref[...] = k.astype(BF16)
    else:
        a3_ref, q_ref, k32_ref, v32_ref, bg_ref, qkvc_ref, cm_ref = out_refs
        q_ref[...] = mm(C_Q, C_K) * (DK_B ** -0.5)
        k = mm(C_K, C_V)
        v = mm(C_V, C_BG)
    a3_ref[...] = mm(C_A3, C_Q)
    tm = x.shape[0]
    for h in range(H_B):
        k32_ref[pl.ds(h, tm, stride=H_B), :] = k[:, h * DV_B:(h + 1) * DV_B]
        v32_ref[pl.ds(h, tm, stride=H_B), :] = v[:, h * DV_B:(h + 1) * DV_B]
    bg_ref[...] = mm(C_BG, C_QKVC)
    qkvc_ref[...] = mm(C_QKVC, C_CG)
    cm_ref[...] = _dot(hn, wtail_ref[...])


def _proj(x, nw, w, wtail, wt, layer, tm, kv_all=None):
    m, d = x.shape
    n_in = w.shape[-1]
    prompt = kv_all is not None
    per_layer = lambda r, c: pl.BlockSpec((None, r, c), lambda i: (layer, 0, 0))
    tok = lambda wd, dt: (pl.BlockSpec((tm, wd), lambda i: (i, 0)), jax.ShapeDtypeStruct((m, wd), dt))
    tr = lambda: (pl.BlockSpec((W_B, tm), lambda i: (0, i)), jax.ShapeDtypeStruct((W_B, m), BF16))
    hd = lambda: (pl.BlockSpec((tm * H_B, DV_B), lambda i: (i, 0)),
                  jax.ShapeDtypeStruct((m * H_B, DV_B), F32))
    hd_all = lambda a: (pl.BlockSpec((None, tm * H_B, DV_B), lambda i: (layer, i, 0)),
                        jax.ShapeDtypeStruct(a.shape, a.dtype))
    in_specs = [pl.BlockSpec((tm, d), lambda i: (i, 0)),
                pl.BlockSpec((1, d), lambda i: (0, 0)),
                per_layer(d, n_in), per_layer(d, W_CM), per_layer(2 * W_B, d)]
    args = [x, nw, w, wtail, wt]
    aliases = {}
    if prompt:
        outs = [tok(3 * W_A, F32), tr(), hd_all(kv_all[0]), tok(W_B, BF16), hd_all(kv_all[1]), tr(),
                tok(W_B, F32), tok(QKV_C, F32), tok(W_CM, F32)]
        in_specs += [pl.BlockSpec(memory_space=pl.ANY)] * 2
        aliases = {len(args): 2, len(args) + 1: 4}
        args += list(kv_all)
    else:
        outs = [tok(3 * W_A, F32), tok(W_B, F32), hd(), hd(),
                tok(W_B, F32), tok(QKV_C, F32), tok(W_CM, F32)]
    return pl.pallas_call(
        functools.partial(_proj_kernel, prompt=prompt),
        grid=(m // tm,),
        in_specs=in_specs,
        out_specs=[o[0] for o in outs],
        out_shape=[o[1] for o in outs],
        input_output_aliases=aliases,
        compiler_params=_cparams(("parallel",)),
        name="proj",
    )(*args)


def _outproj_kernel(ya_ref, yb_ref, yc_ref, h_ref, w_ref, fw_ref, o_ref, *, final):
    ycat = jnp.concatenate([ya_ref[...].astype(BF16), yb_ref[...].astype(BF16),
                            yc_ref[...].astype(BF16)], axis=-1)
    hnew = h_ref[...] + _dot(ycat, w_ref[...])
    if final:
        ms = jnp.mean(hnew * hnew, axis=-1, keepdims=True)
        hnew = hnew * lax.rsqrt(ms + EPS) * fw_ref[...]
    o_ref[...] = hnew


def _outproj(ya, yb, yc, h, w, fw, tm, final):
    m, d = h.shape
    row = lambda wd: pl.BlockSpec((tm, wd), lambda i: (i, 0))
    return pl.pallas_call(
        functools.partial(_outproj_kernel, final=final),
        grid=(m // tm,),
        in_specs=[row(W_A), row(W_B), row(W_C), row(d),
                  pl.BlockSpec((D_MIX, d), lambda i: (0, 0)),
                  pl.BlockSpec((1, d), lambda i: (0, 0))],
        out_specs=row(d),
        out_shape=jax.ShapeDtypeStruct((m, d), F32),
        compiler_params=_cparams(("parallel",)),
        name="outproj",
    )(ya, yb, yc, h, w, fw)


def _conva_tail(conv, a_gate, cb, lnw, lnb, pw, pb):
    y = conv + cb
    mu = jnp.mean(y, axis=-1, keepdims=True)
    yc = y - mu
    y = yc * lax.rsqrt(jnp.mean(yc * yc, axis=-1, keepdims=True) + EPS) * lnw + lnb
    y = _silu(y)
    y = _dot(y.astype(BF16), pw) + pb
    return y * _silu(a_gate)


CONV_PRE = 32
CONV_OFF = CONV_PRE - (CONV_A - 1)
CONV_RC = 64


def _conva_prompt_kernel(a3_ref, cw_ref, cb_ref, lnw_ref, lnb_ref, pw_ref, pb_ref,
                         ya_ref, tail_ref, ext_ref, *, tl):
    i = pl.program_id(1)

    @pl.when(i == 0)
    def _():
        ext_ref[0, 0:CONV_PRE, :] = jnp.zeros((CONV_PRE, W_A), F32)

    u = a3_ref[:, 0:W_A] * _sigmoid(a3_ref[:, W_A:2 * W_A])
    ext_ref[0, CONV_PRE:CONV_PRE + tl, :] = u
    n_shift = CONV_PRE + tl - SUBLANES
    for s in range(1, SUBLANES):
        ext_ref[s, 0:n_shift, :] = ext_ref[0, s:s + n_shift, :]
    for r in range(tl // CONV_RC):
        acc = jnp.zeros((CONV_RC, W_A), F32)
        for j in range(CONV_A):
            off = r * CONV_RC + CONV_OFF + j
            s = off % SUBLANES
            acc = acc + cw_ref[j:j + 1, :] * ext_ref[s, off - s:off - s + CONV_RC, :]
        gate = a3_ref[r * CONV_RC:(r + 1) * CONV_RC, 2 * W_A:3 * W_A]
        y = _conva_tail(acc, gate, cb_ref[...], lnw_ref[...], lnb_ref[...], pw_ref[...], pb_ref[...])
        ya_ref[r * CONV_RC:(r + 1) * CONV_RC, :] = y.astype(BF16)
    last = ext_ref[0, tl:tl + CONV_PRE, :]
    tail_ref[...] = last
    ext_ref[0, 0:CONV_PRE, :] = last


def _conva_prompt(a3, cw, cb, lnw, lnb, pw, pb, bsz, seq, tl):
    nl = seq // tl
    full = lambda shp: pl.BlockSpec(shp, lambda b, i: (0,) * len(shp))
    return pl.pallas_call(
        functools.partial(_conva_prompt_kernel, tl=tl),
        grid=(bsz, nl),
        in_specs=[pl.BlockSpec((tl, 3 * W_A), lambda b, i: (b * nl + i, 0)),
                  full((CONV_A, W_A)), full((1, W_A)), full((1, W_A)), full((1, W_A)),
                  full((W_A, W_A)), full((1, W_A))],
        out_specs=[pl.BlockSpec((tl, W_A), lambda b, i: (b * nl + i, 0)),
                   pl.BlockSpec((None, CONV_PRE, W_A), lambda b, i: (b, 0, 0))],
        out_shape=[jax.ShapeDtypeStruct((bsz * seq, W_A), BF16),
                   jax.ShapeDtypeStruct((bsz, CONV_PRE, W_A), F32)],
        scratch_shapes=[pltpu.VMEM((SUBLANES, CONV_PRE + tl, W_A), F32)],
        compiler_params=_cparams(("parallel", "arbitrary")),
        name="conva_prompt",
    )(a3, cw, cb, lnw, lnb, pw, pb)


SEQ_GROUP = 8


def _conva_sample_kernel(a3_ref, st_ref, cw_ref, cb_ref, lnw_ref, lnb_ref, pw_ref, pb_ref,
                         ya_ref, u_ref, ext_ref):
    a3 = a3_ref[...]
    u = a3[:, :, 0:W_A] * _sigmoid(a3[:, :, W_A:2 * W_A])
    u_ref[...] = u
    ext_ref[:, 0:CONV_PRE, :] = st_ref[...]
    ext_ref[:, CONV_PRE:CONV_PRE + SROWS, :] = u
    acc = jnp.zeros((SEQ_GROUP, SROWS, W_A), F32)
    for j in range(CONV_A):
        acc = acc + cw_ref[j:j + 1, :] * ext_ref[:, CONV_OFF + j:CONV_OFF + j + SROWS, :]
    rows = SEQ_GROUP * SROWS
    y = _conva_tail(acc.reshape(rows, W_A), a3[:, :, 2 * W_A:3 * W_A].reshape(rows, W_A),
                    cb_ref[...], lnw_ref[...], lnb_ref[...], pw_ref[...], pb_ref[...])
    ya_ref[...] = y.reshape(SEQ_GROUP, SROWS, W_A)


def _conva_sample(a3, st, cw, cb, lnw, lnb, pw, pb):
    nb = a3.shape[0]
    full = lambda shp: pl.BlockSpec(shp, lambda g: (0,) * len(shp))
    seqs = lambda r, wd: pl.BlockSpec((SEQ_GROUP, r, wd), lambda g: (g, 0, 0))
    return pl.pallas_call(
        _conva_sample_kernel,
        grid=(nb // SEQ_GROUP,),
        in_specs=[seqs(SROWS, 3 * W_A), seqs(CONV_PRE, W_A),
                  full((CONV_A, W_A)), full((1, W_A)), full((1, W_A)), full((1, W_A)),
                  full((W_A, W_A)), full((1, W_A))],
        out_specs=[seqs(SROWS, W_A), seqs(SROWS, W_A)],
        out_shape=[jax.ShapeDtypeStruct((nb, SROWS, W_A), F32),
                   jax.ShapeDtypeStruct((nb, SROWS, W_A), F32)],
        scratch_shapes=[pltpu.VMEM((SEQ_GROUP, CONV_PRE + SROWS, W_A), F32)],
        compiler_params=_cparams(("parallel",)),
        name="conva_sample",
    )(a3, st, cw, cb, lnw, lnb, pw, pb)


def _lambda_full(lq1, lk1, lq2, lk2, lam_init):
    s1 = jnp.sum(lq1 * lk1, axis=-1, keepdims=True)
    s2 = jnp.sum(lq2 * lk2, axis=-1, keepdims=True)
    return jnp.exp(s1) - jnp.exp(s2) + lam_init


def _subln_gate(o, sub, bg, lam_init):
    y = o * lax.rsqrt(jnp.mean(o * o, axis=-1, keepdims=True) + EPS) * sub
    return (y * (1.0 - lam_init)) * _silu(bg)


ATTN_TILE = 2048
ATTN_STRIP = 256


def _attn_prompt_kernel(qt_ref, k_ref, vt_ref, bg_ref, sub_ref, lq1_ref, lk1_ref, lq2_ref, lk2_ref,
                        o_ref, m1_ref, l1_ref, a1_ref, m2_ref, l2_ref, a2_ref, *, lam_init, t):
    qi = pl.program_id(2)
    qt = qt_ref[...]
    feat = lax.broadcasted_iota(jnp.int32, qt.shape, 0)
    zero = jnp.zeros_like(qt)
    qs = (jnp.where(feat < DK_B, qt, zero), jnp.where(feat >= DK_B, qt, zero))
    stats = ((m1_ref, l1_ref, a1_ref), (m2_ref, l2_ref, a2_ref))
    for m_ref, l_ref, a_ref in stats:
        m_ref[...] = jnp.full(m_ref.shape, -jnp.inf, F32)
        l_ref[...] = jnp.zeros(l_ref.shape, F32)
        a_ref[...] = jnp.zeros(a_ref.shape, F32)

    sw = min(ATTN_STRIP, t)
    units = [(mp, c) for c in range(t // sw) for mp in (0, 1)]

    def tile(j, causal):
        start = pl.multiple_of(j * t, t)
        k = k_ref[pl.ds(start, t), :]
        vt = vt_ref[:, pl.ds(start, t)]
        nkeys = lambda c: (c + 1) * sw if causal else t

        def scores(u):
            mp, c = u
            return _dot(k[:nkeys(c)], qs[mp][:, c * sw:(c + 1) * sw])

        def softmax(u, st):
            mp, c = u
            m_ref, l_ref, _ = stats[mp]
            cols = slice(c * sw, (c + 1) * sw)
            if causal:
                kr = lax.broadcasted_iota(jnp.int32, st.shape, 0)
                qc = lax.broadcasted_iota(jnp.int32, st.shape, 1) + c * sw
                st = jnp.where(kr <= qc, st, -jnp.inf)
            m_prev = m_ref[:, cols]
            m_new = jnp.maximum(m_prev, jnp.max(st, axis=0, keepdims=True))
            alpha = jnp.exp2(m_prev - m_new)
            pt = jnp.exp2(st - m_new)
            l_ref[:, cols] = alpha * l_ref[:, cols] + jnp.sum(pt, axis=0, keepdims=True)
            m_ref[:, cols] = m_new
            return pt.astype(BF16), alpha

        def weighted_values(u, pt, alpha):
            mp, c = u
            a_ref = stats[mp][2]
            cols = slice(c * sw, (c + 1) * sw)
            a_ref[:, cols] = alpha * a_ref[:, cols] + _dot(vt[:, :nkeys(c)], pt)

        n = len(units)
        sts, pts = {}, {}
        for i in range(n + 2):
            if i < n:
                sts[i] = scores(units[i])
            if 0 <= i - 1 < n:
                pts[i - 1] = softmax(units[i - 1], sts.pop(i - 1))
            if 0 <= i - 2 < n:
                weighted_values(units[i - 2], *pts.pop(i - 2))

    def body(j, carry):
        tile(j, False)
        return carry

    lax.fori_loop(0, qi, body, 0)
    tile(qi, True)
    lam = _lambda_full(lq1_ref[...], lk1_ref[...], lq2_ref[...], lk2_ref[...], lam_init)
    ot = a1_ref[...] / l1_ref[...] - lam * (a2_ref[...] / l2_ref[...])
    o_ref[...] = _subln_gate(ot.T, sub_ref[...], bg_ref[...], lam_init).astype(BF16)


def _attn_prompt(qt, k, vt, bg, sub, lq1, lk1, lq2, lk2, lam_init, bsz, seq, t):
    n = seq // t
    full = lambda shp: pl.BlockSpec(shp, lambda b, h, qi: (0,) * len(shp))
    return pl.pallas_call(
        functools.partial(_attn_prompt_kernel, lam_init=lam_init, t=t),
        grid=(bsz, H_B, n),
        in_specs=[pl.BlockSpec((DV_B, t), lambda b, h, qi: (h, b * n + qi)),
                  pl.BlockSpec((seq, DV_B), lambda b, h, qi: (b, h)),
                  pl.BlockSpec((DV_B, seq), lambda b, h, qi: (h, b)),
                  pl.BlockSpec((t, DV_B), lambda b, h, qi: (b * n + qi, h)),
                  full((1, DV_B)), full((1, DK_B)), full((1, DK_B)), full((1, DK_B)),
                  full((1, DK_B))],
        out_specs=pl.BlockSpec((t, DV_B), lambda b, h, qi: (b * n + qi, h)),
        out_shape=jax.ShapeDtypeStruct((bsz * seq, W_B), BF16),
        scratch_shapes=[pltpu.VMEM((1, t), F32), pltpu.VMEM((1, t), F32), pltpu.VMEM((DV_B, t), F32),
                        pltpu.VMEM((1, t), F32), pltpu.VMEM((1, t), F32), pltpu.VMEM((DV_B, t), F32)],
        compiler_params=_cparams(("parallel", "parallel", "arbitrary")),
        name="attn_prompt",
    )(qt, k, vt, bg, sub, lq1, lk1, lq2, lk2)


PAGES_PER_STEP = 32
N_QROWS = 2 * SROWS


def _attn_sample_kernel(pt_ref, q_ref, *refs, lam_init, n_steps, npg):
    kp_refs = refs[:npg]
    vp_refs = refs[npg:2 * npg]
    (kn_ref, vn_ref, bg_ref, sub_ref, lq1_ref, lk1_ref, lq2_ref, lk2_ref, o_ref,
     qb_ref, m_ref, l_ref, acc_ref) = refs[2 * npg:]
    p = pl.program_id(1)
    heads = range(H_B)

    @pl.when(p == 0)
    def _():
        q = q_ref[...]
        lane = lax.broadcasted_iota(jnp.int32, (SROWS, DV_B), 1)
        for h in heads:
            qh = q[:, h * DV_B:(h + 1) * DV_B]
            qb_ref[h, 0:SROWS, :] = jnp.where(lane < DK_B, qh, 0.0)
            qb_ref[h, SROWS:N_QROWS, :] = jnp.where(lane >= DK_B, qh, 0.0)
        m_ref[...] = jnp.full(m_ref.shape, -jnp.inf, F32)
        l_ref[...] = jnp.zeros(l_ref.shape, F32)
        acc_ref[...] = jnp.zeros(acc_ref.shape, F32)

    def update_all(ss, vs):
        m_prev = [m_ref[h] for h in heads]
        m_new = [jnp.maximum(mp, jnp.max(s, axis=-1, keepdims=True)) for mp, s in zip(m_prev, ss)]
        alpha = [jnp.exp(mp - mn) for mp, mn in zip(m_prev, m_new)]
        pr = [jnp.exp(s - mn) for s, mn in zip(ss, m_new)]
        pv = [_dot(x.astype(BF16), v) for x, v in zip(pr, vs)]
        for h in heads:
            l_ref[h] = alpha[h] * l_ref[h] + jnp.sum(pr[h], axis=-1, keepdims=True)
            acc_ref[h] = alpha[h] * acc_ref[h] + pv[h]
            m_ref[h] = m_new[h]

    def head_rows(page_refs, h):
        return jnp.concatenate([r[pl.ds(h, PAGE, stride=H_B), :].astype(BF16) for r in page_refs], axis=0)

    ks = [head_rows(kp_refs, h) for h in heads]
    ss = [_dot_nt(qb_ref[h].astype(BF16), ks[h]) for h in heads]
    update_all(ss, [head_rows(vp_refs, h) for h in heads])

    @pl.when(p == n_steps - 1)
    def _():
        ss, vs = [], []
        for h in heads:
            kn = kn_ref[pl.ds(h, SROWS, stride=H_B), :].astype(BF16)
            s = _dot_nt(qb_ref[h].astype(BF16), kn)
            tq = lax.broadcasted_iota(jnp.int32, s.shape, 0) % SROWS
            tk = lax.broadcasted_iota(jnp.int32, s.shape, 1)
            ss.append(jnp.where(tk <= tq, s, -jnp.inf))
            vs.append(vn_ref[pl.ds(h, SROWS, stride=H_B), :].astype(BF16))
        update_all(ss, vs)
        lam = _lambda_full(lq1_ref[...], lk1_ref[...], lq2_ref[...], lk2_ref[...], lam_init)
        outs = []
        for h in heads:
            an = acc_ref[h] / l_ref[h]
            o = an[0:SROWS] - lam * an[SROWS:N_QROWS]
            outs.append(_subln_gate(o, sub_ref[...], bg_ref[:, h * DV_B:(h + 1) * DV_B], lam_init))
        o_ref[...] = jnp.concatenate(outs, axis=-1)


def _attn_sample(page_table, q, cache_k, cache_v, kn, vn, bg, sub, lq1, lk1, lq2, lk2, lam_init,
                 layer, npg):
    nb, n_pages = page_table.shape
    n_steps = n_pages // npg
    seq = lambda b, p, pt: (b, 0)

    def page(i):
        return lambda b, p, pt: (layer, pt[b, p * npg + i], 0, 0)

    full = lambda shp: pl.BlockSpec(shp, lambda b, p, pt: (0,) * len(shp))
    pages = [pl.BlockSpec((None, None, PAGE * H_B, DV_B), page(i)) for i in range(npg)]
    tok = pl.BlockSpec((SROWS, W_B), seq)
    new = pl.BlockSpec((SROWS * H_B, DV_B), seq)
    grid_spec = pltpu.PrefetchScalarGridSpec(
        num_scalar_prefetch=1,
        grid=(nb, n_steps),
        in_specs=[tok] + pages + pages + [new, new, tok, full((1, DV_B)), full((1, DK_B)),
                                          full((1, DK_B)), full((1, DK_B)), full((1, DK_B))],
        out_specs=tok,
        scratch_shapes=[pltpu.VMEM((H_B, N_QROWS, DV_B), F32), pltpu.VMEM((H_B, N_QROWS, 1), F32),
                        pltpu.VMEM((H_B, N_QROWS, 1), F32), pltpu.VMEM((H_B, N_QROWS, DV_B), F32)],
    )
    return pl.pallas_call(
        functools.partial(_attn_sample_kernel, lam_init=lam_init, n_steps=n_steps, npg=npg),
        grid_spec=grid_spec,
        out_shape=jax.ShapeDtypeStruct((nb * SROWS, W_B), F32),
        compiler_params=_cparams(("parallel", "arbitrary")),
        name="attn_sample",
    )(page_table, q, *([cache_k] * npg), *([cache_v] * npg), kn, vn, bg, sub, lq1, lk1, lq2, lk2)


def _split(x):
    hi = x.astype(BF16)
    return hi, (x - hi.astype(F32)).astype(BF16)


def _dot3(a, b):
    ah, al = a
    bh, bl = b
    return _dot(ah, bh) + (_dot(ah, bl) + _dot(al, bh))


def _halves(x, cut):
    lane = lax.broadcasted_iota(jnp.int32, x.shape, x.ndim - 1)
    z = jnp.zeros_like(x)
    return jnp.where(lane < cut, x, z), jnp.where(lane >= cut, x, z)


def _blockdiag(x, cut):
    return jnp.concatenate(_halves(x, cut), axis=0)


def _blockdiag_split(xs, cut):
    return _blockdiag(xs[0], cut), _blockdiag(xs[1], cut)


def _pair_lanes(x0, x1, width, cut):
    lane = lax.broadcasted_iota(jnp.int32, (x0.shape[0], width), 1)
    return jnp.where(lane < cut, x0, x1)


def _tri_inverse_many(a_list, c):
    row = lax.broadcasted_iota(jnp.int32, (c, 2 * c), 0)
    lane = lax.broadcasted_iota(jnp.int32, (c, 2 * c), 1)
    col = jnp.where(lane >= c, lane - c, lane)
    eye = (row == col).astype(F32)
    nb = min(c, 16)
    bd = lambda xs: _blockdiag_split(xs, c)
    ad = [jnp.where(row // nb == col // nb, a, 0.0) for a in a_list]
    t = [eye - x for x in ad]
    pw = ad
    size = 2
    while size < nb:
        ps = _each(_split, pw)
        pw = _each(_dot3, ps, _each(bd, ps))
        tp = _each(_dot3, _each(_split, t), _each(bd, _each(_split, pw)))
        t = _each(lambda x, y: x + y, t, tp)
        size *= 2
    size = nb * 2
    while size <= c:
        off = jnp.logical_and(row // size == col // size, row // (size // 2) != col // (size // 2))
        ts = _each(_split, t)
        tb = _each(bd, ts)
        inner = _each(_dot3, [_split(jnp.where(off, a, 0.0)) for a in a_list], tb)
        outer = _each(_dot3, ts, _each(bd, _each(_split, inner)))
        t = _each(lambda x, y: x - y, t, outer)
        size *= 2
    return t


def _delta_prep_many(items, c):
    dh = DK_C
    row = lax.broadcasted_iota(jnp.int32, (c, 2 * c), 0)
    lane = lax.broadcasted_iota(jnp.int32, (c, 2 * c), 1)
    col = jnp.where(lane >= c, lane - c, lane)
    decay = [jnp.exp(jnp.where(row >= col, it[3] - it[5], -jnp.inf)) for it in items]
    kb = [it[1].astype(BF16) for it in items]
    kbd = [_blockdiag(x, dh) for x in kb]
    kk = _each(_dot_nt, kb, kbd)
    a = [jnp.where(row > col, it[6] * x * d, 0.0) for it, x, d in zip(items, kk, decay)]
    t = _tri_inverse_many(a, c)
    eg = [jnp.exp(it[4]) for it in items]
    rhs = []
    for it, e in zip(items, eg):
        vb = _split(it[2] * it[7])
        kbe = _split(it[1] * (it[7] * e))
        rhs.append(tuple(jnp.concatenate([_blockdiag(x, dh), _blockdiag(y, dh)], axis=-1)
                         for x, y in zip(vb, kbe)))
    uw = _each(_dot3, _each(_split, t), rhs)
    qk = _each(_dot_nt, [it[0].astype(BF16) for it in items], kbd)
    out = []
    for it, x, e, s, d in zip(items, uw, eg, qk, decay):
        q, k, gf = it[0], it[1], it[4]
        glast = gf[c - 1:c, :]
        wq = jnp.concatenate([x[:, 2 * dh:], q * e], axis=0).astype(BF16)
        kt = (k * jnp.exp(glast - gf)).astype(BF16)
        out.append((x[:, :2 * dh], wq, (s * d).astype(BF16), kt, jnp.exp(glast)))
    return out


def _delta_scan(u, wq, qk, kt, gl, s, c):
    r = _dot(wq, _blockdiag(s.astype(BF16), DV_C))
    vnb = (u - r[:c]).astype(BF16)
    o = r[c:] + _dot(qk, _blockdiag(vnb, DV_C))
    full = _dot_tn(kt, vnb)
    lane = lax.broadcasted_iota(jnp.int32, s.shape, 1)
    return o, s * gl + jnp.where(lane < DV_C, full[:DK_C], full[DK_C:])


def _pair_norm(x, cut):
    lo, hi = _halves(x * x, cut)
    lane = lax.broadcasted_iota(jnp.int32, x.shape, 1)
    return jnp.where(lane < cut, jnp.sum(lo, axis=-1, keepdims=True),
                     jnp.sum(hi, axis=-1, keepdims=True))


N_PAIRS = H_C // 2


def _gdn_block(x, cm, alog, dtb, onw, states, c, nseg, chained, valid):
    rows = nseg * c
    pw_ = 2 * DK_C
    beta = _sigmoid(cm[:, W_C:W_C + LANES])
    g = -jnp.exp(alog) * _softplus(cm[:, W_C + LANES:W_C + 2 * LANES] + dtb)
    if valid is not None:
        beta = jnp.where(valid, beta, 0.0)
        g = jnp.where(valid, g, 0.0)
        x = jnp.where(valid, x, 0.0)
    row = lax.broadcasted_iota(jnp.int32, (rows, rows), 0)
    col = lax.broadcasted_iota(jnp.int32, (rows, rows), 1)
    same = row // c == col // c
    lower = jnp.logical_and(same, row >= col).astype(F32).astype(BF16)
    upper = jnp.logical_and(same, row <= col).astype(F32).astype(BF16)
    g1 = g.astype(BF16)
    r1 = g - g1.astype(F32)
    g2 = r1.astype(BF16)
    g3 = (r1 - g2.astype(F32)).astype(BF16)
    gcs = _dot(lower, g1) + (_dot(lower, g2) + _dot(lower, g3))
    gcs_t = _dot_tn(g1, upper) + (_dot_tn(g2, upper) + _dot_tn(g3, upper))
    qs, ks = [], []
    for p in range(N_PAIRS):
        xq = x[:, p * pw_:(p + 1) * pw_]
        xk = x[:, W_C + p * pw_:W_C + (p + 1) * pw_]
        qs.append(xq * lax.rsqrt(_pair_norm(xq, DK_C) + EPS) * (DK_C ** -0.5))
        ks.append(xk * lax.rsqrt(_pair_norm(xk, DK_C) + EPS))
    keys = [(i, p) for i in range(nseg) for p in range(N_PAIRS)]
    seg = lambda a, i: a[i * c:(i + 1) * c]
    items = []
    for i, p in keys:
        h0, h1 = 2 * p, 2 * p + 1
        g0, g1 = seg(gcs, i)[:, h0:h0 + 1], seg(gcs, i)[:, h1:h1 + 1]
        b0, b1 = seg(beta, i)[:, h0:h0 + 1], seg(beta, i)[:, h1:h1 + 1]
        grow = jnp.concatenate([gcs_t[h0:h0 + 1, i * c:(i + 1) * c],
                                gcs_t[h1:h1 + 1, i * c:(i + 1) * c]], axis=-1)
        items.append((seg(qs[p], i), seg(ks[p], i),
                      seg(x, i)[:, 2 * W_C + p * pw_:2 * W_C + (p + 1) * pw_],
                      _pair_lanes(g0, g1, 2 * c, c), _pair_lanes(g0, g1, pw_, DK_C), grow,
                      _pair_lanes(b0, b1, 2 * c, c), _pair_lanes(b0, b1, pw_, DK_C)))
    preps = dict(zip(keys, _delta_prep_many(items, c)))
    onw2 = jnp.concatenate([onw, onw], axis=-1)
    s_cur = list(states)
    ys = []
    for i in range(nseg):
        outs = []
        for p in range(N_PAIRS):
            si = p if chained else i * N_PAIRS + p
            o, s_cur[si] = _delta_scan(*preps[i, p], s_cur[si], c)
            o = o * lax.rsqrt(_pair_norm(o, DV_C) * (1.0 / DV_C) + EPS) * onw2
            outs.append(o * _silu(seg(cm, i)[:, p * pw_:(p + 1) * pw_]))
        ys.append(jnp.concatenate(outs, axis=-1))
    return jnp.concatenate(ys, axis=0), s_cur


def _pack_states(per_head):
    return [jnp.concatenate([per_head[2 * p], per_head[2 * p + 1]], axis=-1) for p in range(N_PAIRS)]


def _unpack_state(packed, h):
    return packed[h // 2][:, (h % 2) * DV_C:(h % 2 + 1) * DV_C]


CONVC_PRE = SUBLANES
CONVC_OFF = CONVC_PRE - (SHORT_CONV - 1)
GDN_CHUNKS = 8


def _gdn_prompt_kernel(x_ref, cm_ref, cw_ref, alog_ref, dtb_ref, onw_ref, y_ref, s_ref,
                       ext_ref, st_ref, *, nch):
    i = pl.program_id(1)
    rows = nch * CHUNK

    @pl.when(i == 0)
    def _():
        ext_ref[0:CONVC_PRE, :] = jnp.zeros((CONVC_PRE, QKV_C), F32)
        st_ref[...] = jnp.zeros(st_ref.shape, F32)

    ext_ref[CONVC_PRE:CONVC_PRE + rows, :] = x_ref[...]
    acc = jnp.zeros((rows, QKV_C), F32)
    for j in range(SHORT_CONV):
        acc = acc + cw_ref[j:j + 1, :] * ext_ref[CONVC_OFF + j:CONVC_OFF + j + rows, :]
    ext_ref[0:CONVC_PRE, :] = ext_ref[rows:rows + CONVC_PRE, :]
    y, s_new = _gdn_block(_silu(acc), cm_ref[...], alog_ref[...], dtb_ref[...], onw_ref[...],
                          [st_ref[p] for p in range(N_PAIRS)], CHUNK, nch, True, None)
    y_ref[...] = y.astype(BF16)
    for p in range(N_PAIRS):
        st_ref[p] = s_new[p]

    @pl.when(i == pl.num_programs(1) - 1)
    def _():
        for h in range(H_C):
            s_ref[h] = _unpack_state(s_new, h)


def _gdn_prompt(x, cm, cw, alog, dtb, onw, bsz, seq, nch):
    rows = nch * CHUNK
    nc = seq // rows
    full = lambda shp: pl.BlockSpec(shp, lambda b, i: (0,) * len(shp))
    tok = lambda wd: pl.BlockSpec((rows, wd), lambda b, i: (b * nc + i, 0))
    return pl.pallas_call(
        functools.partial(_gdn_prompt_kernel, nch=nch),
        grid=(bsz, nc),
        in_specs=[tok(QKV_C), tok(W_CM), full((SHORT_CONV, QKV_C)), full((1, LANES)),
                  full((1, LANES)), full((1, DV_C))],
        out_specs=[tok(W_C), pl.BlockSpec((None, H_C, DK_C, DV_C), lambda b, i: (b, 0, 0, 0))],
        out_shape=[jax.ShapeDtypeStruct((bsz * seq, W_C), BF16),
                   jax.ShapeDtypeStruct((bsz, H_C, DK_C, DV_C), F32)],
        scratch_shapes=[pltpu.VMEM((CONVC_PRE + rows, QKV_C), F32),
                        pltpu.VMEM((N_PAIRS, DK_C, 2 * DV_C), F32)],
        compiler_params=_cparams(("parallel", "arbitrary")),
        name="gdn_prompt",
    )(x, cm, cw, alog, dtb, onw)


def _gdn_sample_kernel(x_ref, st_ref, cm_ref, s0_ref, cw_ref, alog_ref, dtb_ref, onw_ref,
                       y_ref, s_ref, *, n_real):
    accs = []
    for g in range(SEQ_GROUP):
        ext = jnp.concatenate([st_ref[g], x_ref[g * SROWS:(g + 1) * SROWS, :]], axis=0)
        acc = jnp.zeros((SROWS, QKV_C), F32)
        for j in range(SHORT_CONV):
            acc = acc + cw_ref[j:j + 1, :] * ext[CONVC_OFF + j:CONVC_OFF + j + SROWS, :]
        accs.append(acc)
    rows = SEQ_GROUP * SROWS
    valid = lax.broadcasted_iota(jnp.int32, (rows, 1), 0) % SROWS < n_real
    states = [sp for g in range(SEQ_GROUP)
              for sp in _pack_states([s0_ref[g, h] for h in range(H_C)])]
    y, s_new = _gdn_block(_silu(jnp.concatenate(accs, axis=0)), cm_ref[...], alog_ref[...],
                          dtb_ref[...], onw_ref[...], states, SROWS, SEQ_GROUP, False, valid)
    y_ref[...] = y
    for g in range(SEQ_GROUP):
        for h in range(H_C):
            s_ref[g, h] = _unpack_state(s_new[g * N_PAIRS:(g + 1) * N_PAIRS], h)


def _gdn_sample(x, st, cm, s0, cw, alog, dtb, onw, n_real):
    nb = s0.shape[0]
    rows = SEQ_GROUP * SROWS
    full = lambda shp: pl.BlockSpec(shp, lambda g: (0,) * len(shp))
    tok = lambda wd: pl.BlockSpec((rows, wd), lambda g: (g, 0))
    state = pl.BlockSpec((SEQ_GROUP, H_C, DK_C, DV_C), lambda g: (g, 0, 0, 0))
    return pl.pallas_call(
        functools.partial(_gdn_sample_kernel, n_real=n_real),
        grid=(nb // SEQ_GROUP,),
        in_specs=[tok(QKV_C), pl.BlockSpec((SEQ_GROUP, CONVC_PRE, QKV_C), lambda g: (g, 0, 0)),
                  tok(W_CM), state, full((SHORT_CONV, QKV_C)), full((1, LANES)), full((1, LANES)),
                  full((1, DV_C))],
        out_specs=[tok(W_C), state],
        out_shape=[jax.ShapeDtypeStruct((nb * SROWS, W_C), F32),
                   jax.ShapeDtypeStruct((nb, H_C, DK_C, DV_C), F32)],
        compiler_params=_cparams(("parallel",)),
        name="gdn_sample",
    )(x, st, cm, s0, cw, alog, dtb, onw)


def _prepare_w_in(w_in):
    depth, d, _ = w_in.shape
    beta = w_in[:, :, C_CG:C_CG + H_C]
    alpha = w_in[:, :, C_CG + H_C:C_CG + 2 * H_C]
    gate = w_in[:, :, C_CG + 2 * H_C:]
    zpad = jnp.zeros((depth, d, LANES - H_C), w_in.dtype)
    wtail = jnp.concatenate([gate, beta, zpad, alpha, zpad], axis=-1).astype(BF16)
    wt = jnp.concatenate([w_in[:, :, C_Q:C_K], w_in[:, :, C_V:C_BG]], axis=-1)
    return w_in.astype(BF16), wtail, jnp.swapaxes(wt, 1, 2).astype(BF16)


def _pad_lanes(x):
    return jnp.pad(x, ((0, 0), (0, LANES - x.shape[-1])))


def kernel(x_prompt, x_sample, cache_k, cache_v, state_conv_a, state_conv_c, state_delta, page_table,
           norm_w, w_in, conv_a_w, conv_a_b, ln_a_w, ln_a_b, pw_a_w, pw_a_b,
           lambda_q1, lambda_k1, lambda_q2, lambda_k2, subln_w, conv_c_w, a_log, dt_bias,
           onorm_c_w, w_out, final_norm_w):
    bsz, seq, d = x_prompt.shape
    nb, ns, _ = x_sample.shape
    depth = w_in.shape[0]
    n_pool = cache_k.shape[1]
    n_pages = page_table.shape[1]
    t_attn = ATTN_TILE if seq % ATTN_TILE == 0 else seq
    tl_conv = 512 if seq % 512 == 0 else seq
    tm_p = 256
    tm_o = 1024 if (bsz * seq) % 1024 == 0 else tm_p
    nch = GDN_CHUNKS if seq % (GDN_CHUNKS * CHUNK) == 0 else 1
    npg = PAGES_PER_STEP if n_pages % PAGES_PER_STEP == 0 else 1
    assert SHORT_CONV - 1 <= ns <= SROWS, "sample tokens must fit one 8-row tile"
    assert nb % SEQ_GROUP == 0 and seq % CHUNK == 0 and seq % t_attn == 0 and (bsz * seq) % tm_p == 0
    assert tl_conv % CONV_RC == 0 and tl_conv >= CONV_PRE

    wp, wtail, wt = _prepare_w_in(w_in)
    wo = w_out.astype(BF16)
    pw = pw_a_w.astype(BF16)
    ck = cache_k.reshape(depth, n_pool, PAGE * H_B, DV_B)
    cv = cache_v.reshape(depth, n_pool, PAGE * H_B, DV_B)
    st_a = jnp.pad(state_conv_a, ((0, 0), (0, 0), (CONV_PRE - (CONV_A - 1), 0), (0, 0)))
    st_c = jnp.pad(state_conv_c, ((0, 0), (0, 0), (CONVC_PRE - (SHORT_CONV - 1), 0), (0, 0)))
    alog = _pad_lanes(a_log)
    dtb = _pad_lanes(dt_bias)
    fw = final_norm_w.reshape(1, d)

    hp = x_prompt.reshape(bsz * seq, d)
    hs = jnp.pad(x_sample, ((0, 0), (0, SROWS - ns), (0, 0))).reshape(nb * SROWS, d)
    outs_p = [[] for _ in range(5)]
    outs_s = [[] for _ in range(5)]
    kv_all = tuple(jnp.zeros((depth, bsz * seq * H_B, DV_B), F32) for _ in range(2))
    for l in range(depth):
        lam_init = 0.8 - 0.6 * math.exp(-0.3 * l)
        final = l == depth - 1
        row = lambda a: a[l].reshape(1, -1)
        lams = (row(lambda_q1), row(lambda_k1), row(lambda_q2), row(lambda_k2))

        a3, qt, k_all, k16, v_all, vt, bg, qkvc, cm = _proj(hp, row(norm_w), wp, wtail, wt, l, tm_p,
                                                            kv_all)
        kv_all = (k_all, v_all)
        ya, tail = _conva_prompt(a3, conv_a_w[l], row(conv_a_b), row(ln_a_w), row(ln_a_b), pw[l],
                                 row(pw_a_b), bsz, seq, tl_conv)
        yb = _attn_prompt(qt, k16, vt, bg, row(subln_w), *lams, lam_init, bsz, seq, t_attn)
        yc, s_new = _gdn_prompt(qkvc, cm, conv_c_w[l], alog[l:l + 1], dtb[l:l + 1], row(onorm_c_w),
                                bsz, seq, nch)
        hp = _outproj(ya, yb, yc, hp, wo[l], fw, tm_o, final)
        outs_p[2].append(tail[:, CONV_PRE - (CONV_A - 1):])
        outs_p[3].append(qkvc.reshape(bsz, seq, QKV_C)[:, seq - (SHORT_CONV - 1):])
        outs_p[4].append(s_new)

        a3, q, k32, v32, bg, qkvc, cm = _proj(hs, row(norm_w), wp, wtail, wt, l, nb * SROWS)
        ya, u = _conva_sample(a3.reshape(nb, SROWS, 3 * W_A), st_a[l], conv_a_w[l], row(conv_a_b),
                              row(ln_a_w), row(ln_a_b), pw[l], row(pw_a_b))
        yb = _attn_sample(page_table, q, ck, cv, k32, v32, bg, row(subln_w), *lams, lam_init, l, npg)
        yc, s_new = _gdn_sample(qkvc, st_c[l], cm, state_delta[l], conv_c_w[l], alog[l:l + 1],
                                dtb[l:l + 1], row(onorm_c_w), ns)
        hs = _outproj(ya.reshape(nb * SROWS, W_A), yb, yc, hs, wo[l], fw, nb * SROWS, final)
        outs_s[0].append(k32.reshape(nb, SROWS, H_B, 2 * DK_B)[:, :ns])
        outs_s[1].append(v32.reshape(nb, SROWS, H_B, DV_B)[:, :ns])
        outs_s[2].append(jnp.concatenate([state_conv_a[l], u[:, :ns]], axis=1)[:, -(CONV_A - 1):])
        outs_s[3].append(jnp.concatenate([state_conv_c[l], qkvc.reshape(nb, SROWS, QKV_C)[:, :ns]],
                                         axis=1)[:, -(SHORT_CONV - 1):])
        outs_s[4].append(s_new)

    y_prompt = hp.reshape(bsz, seq, d)
    y_sample = hs.reshape(nb, SROWS, d)[:, :ns]
    stack = lambda xs: jnp.stack(xs)
    return (y_prompt, y_sample,
            kv_all[0].reshape(depth, bsz, seq, H_B, 2 * DK_B), kv_all[1].reshape(depth, bsz, seq, H_B, DV_B),
            stack(outs_p[2]), stack(outs_p[3]),
            stack(outs_p[4]).astype(state_delta.dtype),
            stack(outs_s[0]), stack(outs_s[1]), stack(outs_s[2]), stack(outs_s[3]),
            stack(outs_s[4]).astype(state_delta.dtype))
```
